```python
import jax, jax.numpy as jnp
from jax import lax
import numpy as np

D_MODEL = 4096
BATCH = 4
SEQ = 2048
DEPTH = 2
DEC_BATCH = 128
DEC_SEQ = 4
PAST_LEN = 16384
PAGE_SIZE = 128

N_A_LAYERS = DEPTH // 2
N_B_LAYERS = DEPTH - N_A_LAYERS
HG_DK = 128
HG_HEADS = D_MODEL // HG_DK
HG_DV = D_MODEL // HG_HEADS
HG_CHUNK = 32
MLA_HEADS = 32
Q_LORA = 896
KV_LORA = 512
NOPE_DIM = 128
ROPE_DIM = 64
V_DIM = 128
ROPE_THETA = 10000.0
Q_BLOCK = 128
ATTN_SCALE = (NOPE_DIM + ROPE_DIM) ** -0.5
D_FF = 11008
CONV_W = 3
PLE_DIM = 256
EPS = 1e-6
F32 = jnp.float32

kernel_name = 'yoco_hgrn2_mla_convffn_ple_step'


def rmsnorm(x, g):
    xf = x.astype(F32)
    y = xf * lax.rsqrt(jnp.mean(xf * xf, axis=-1, keepdims=True) + EPS)
    return (y * g.astype(F32)).astype(x.dtype)


def rope(x, pos):
    half = x.shape[-1] // 2
    inv = ROPE_THETA ** (-jnp.arange(half, dtype=F32) / half)
    ang = pos.astype(F32)[:, None] * inv[None, :]
    shape = (1, pos.shape[0]) + (1,) * (x.ndim - 3) + (half,)
    cos = jnp.cos(ang).reshape(shape)
    sin = jnp.sin(ang).reshape(shape)
    xf = x.astype(F32)
    x1, x2 = xf[..., :half], xf[..., half:]
    return jnp.concatenate([x1 * cos - x2 * sin, x1 * sin + x2 * cos], axis=-1).astype(x.dtype)


def gla_chunkwise(q, k, v, g, s0):
    bsz, t, nh, _ = q.shape
    dv = v.shape[-1]
    n_chunks = -(-t // HG_CHUNK)
    pad = n_chunks * HG_CHUNK - t
    padw = ((0, 0), (0, pad), (0, 0), (0, 0))
    q, k, v, g = (jnp.pad(a, padw) for a in (q, k, v, g))

    def to_chunks(a):
        return a.reshape(bsz, n_chunks, HG_CHUNK, nh, a.shape[-1]).transpose(1, 0, 3, 2, 4)

    causal = jnp.tril(jnp.ones((HG_CHUNK, HG_CHUNK), dtype=bool))

    def step(s, blk):
        qb, kb, vb, gb = blk
        b = jnp.cumsum(gb, axis=2)
        b_end = b[:, :, -1:, :]
        q_dec = qb * jnp.exp(b)
        k_inc = kb * jnp.exp(-b)
        a = jnp.where(causal, jnp.einsum('bhtd,bhsd->bhts', q_dec, k_inc), 0.0)
        o = jnp.einsum('bhtd,bhdv->bhtv', q_dec, s) + jnp.einsum('bhts,bhsv->bhtv', a, vb)
        s_new = jnp.exp(b_end[:, :, 0, :, None]) * s + jnp.einsum('bhsd,bhsv->bhdv', kb * jnp.exp(b_end - b), vb)
        return s_new, o

    s_fin, o = lax.scan(step, s0, tuple(to_chunks(a) for a in (q, k, v, g)))
    o = o.transpose(1, 0, 3, 2, 4).reshape(bsz, n_chunks * HG_CHUNK, nh, dv)[:, :t]
    return o, s_fin


def hgrn2_mixer(xn, w_q, w_f, w_i, w_g, onorm_g, w_o, lb, s0):
    bsz, t, _ = xn.shape
    q = jax.nn.silu(xn @ w_q).astype(F32).reshape(bsz, t, HG_HEADS, HG_DK)
    lbh = lb.astype(F32).reshape(HG_HEADS, HG_DK)
    f = lbh + (1.0 - lbh) * jax.nn.sigmoid((xn @ w_f).astype(F32).reshape(bsz, t, HG_HEADS, HG_DK))
    i = (xn @ w_i).astype(F32).reshape(bsz, t, HG_HEADS, HG_DV)
    o, s_fin = gla_chunkwise(q, 1.0 - f, i, jnp.log(f), s0.astype(F32))
    o = rmsnorm(o, onorm_g.reshape(HG_HEADS, HG_DV)).reshape(bsz, t, HG_HEADS * HG_DV).astype(xn.dtype)
    o = o * jax.nn.silu(xn @ w_g)
    return o @ w_o, s_fin.astype(s0.dtype)


def conv_ffn(xn, w_gate, w_up, conv_w, conv_b, w_down, buf):
    a = xn @ w_gate
    u = xn @ w_up
    t = a.shape[1]
    ext = jnp.concatenate([buf.astype(a.dtype), a], axis=1)
    conv = conv_b
    for j in range(CONV_W):
        conv = conv + ext[:, j:j + t] * conv_w[j]
    y = (jax.nn.gelu(conv, approximate=False) * u) @ w_down
    return y, ext[:, -(CONV_W - 1):].astype(buf.dtype)


def ple_add(h, p, norm_g, w_pg, w_pe):
    gate = jax.nn.sigmoid(rmsnorm(h, norm_g) @ w_pg)
    return h + gate * (p.astype(h.dtype) @ w_pe)


def mla_kv_stream(h, kv_norm, w_dkv, ckv_norm, pos):
    ckr = rmsnorm(h, kv_norm) @ w_dkv
    return rmsnorm(ckr[..., :KV_LORA], ckv_norm), rope(ckr[..., KV_LORA:], pos)


def mla_queries(xn, w_dq, q_norm, w_uq, pos):
    bsz, t, _ = xn.shape
    q = (rmsnorm(xn @ w_dq, q_norm) @ w_uq).reshape(bsz, t, MLA_HEADS, NOPE_DIM + ROPE_DIM)
    return q[..., :NOPE_DIM], rope(q[..., NOPE_DIM:], pos)


def mla_prompt_attention(q_nope, q_pe, c, kpe, w_uk, w_uv):
    bsz, t, nh, _ = q_nope.shape
    k_nope = jnp.einsum('bsc,chd->bshd', c, w_uk)
    v = jnp.einsum('bsc,chd->bshd', c, w_uv)
    qb = Q_BLOCK if t % Q_BLOCK == 0 else t
    nb = t // qb
    qn = q_nope.reshape(bsz, nb, qb, nh, NOPE_DIM).transpose(1, 0, 2, 3, 4)
    qp = q_pe.reshape(bsz, nb, qb, nh, ROPE_DIM).transpose(1, 0, 2, 3, 4)
    kpos = jnp.arange(t)

    def block(args):
        bi, qn_b, qp_b = args
        s = (jnp.einsum('bqhd,bshd->bhqs', qn_b, k_nope).astype(F32)
             + jnp.einsum('bqhr,bsr->bhqs', qp_b, kpe).astype(F32)) * ATTN_SCALE
        qpos = bi * qb + jnp.arange(qb)
        s = jnp.where(kpos[None, :] <= qpos[:, None], s, -jnp.inf)
        p = jax.nn.softmax(s, axis=-1).astype(v.dtype)
        return jnp.einsum('bhqs,bshd->bqhd', p, v)

    o = lax.map(block, (jnp.arange(nb), qn, qp))
    return o.transpose(1, 0, 2, 3, 4).reshape(bsz, t, nh, V_DIM)


def mla_sample_attention(q_nope, q_pe, c_new, kpe_new, cache_ckv, cache_kpe, page_table, w_uk, w_uv):
    t = q_nope.shape[1]
    q_lat = jnp.einsum('bthd,chd->bthc', q_nope, w_uk)
    causal = jnp.tril(jnp.ones((t, t), dtype=bool))

    def one(args):
        ql, qp, cn, kn, pages = args
        c_past = cache_ckv[pages].reshape(-1, KV_LORA)
        k_past = cache_kpe[pages].reshape(-1, ROPE_DIM)
        n_past = c_past.shape[0]
        s_past = (jnp.einsum('thc,sc->hts', ql, c_past).astype(F32)
                  + jnp.einsum('thr,sr->hts', qp, k_past).astype(F32))
        s_new = (jnp.einsum('thc,sc->hts', ql, cn).astype(F32)
                 + jnp.einsum('thr,sr->hts', qp, kn).astype(F32))
        s_new = jnp.where(causal, s_new, -jnp.inf)
        p = jax.nn.softmax(jnp.concatenate([s_past, s_new], axis=-1) * ATTN_SCALE, axis=-1).astype(cn.dtype)
        return (jnp.einsum('hts,sc->thc', p[..., :n_past], c_past)
                + jnp.einsum('hts,sc->thc', p[..., n_past:], cn))

    lat = lax.map(one, (q_lat, q_pe, c_new, kpe_new, page_table))
    return jnp.einsum('bthc,chd->bthd', lat, w_uv)


def setup_inputs(seed: int = 0) -> dict:
    key = jax.random.key(seed)
    ks = jax.random.split(key, 40)
    n_pages = PAST_LEN // PAGE_SIZE
    n_used = DEC_BATCH * n_pages
    n_pool = n_used + n_used // 4

    def nrm(i, shape, scale):
        return jax.random.normal(ks[i], shape, F32) * scale

    def gain(i, shape):
        return 1.0 + 0.02 * jax.random.normal(ks[i], shape, F32)

    page_table = jax.random.permutation(ks[6], n_pool)[:n_used].reshape(DEC_BATCH, n_pages).astype(jnp.int32)
    hw = HG_HEADS * HG_DK
    return {
        'x_prompt': nrm(0, (BATCH, SEQ, D_MODEL), 1.0),
        'x_sample': nrm(1, (DEC_BATCH, DEC_SEQ, D_MODEL), 1.0),
        'state_hgrn': nrm(2, (N_A_LAYERS, DEC_BATCH, HG_HEADS, HG_DK, HG_DV), 0.5),
        'state_conv': nrm(3, (DEPTH, DEC_BATCH, CONV_W - 1, D_FF), 1.0),
        'cache_ckv': nrm(4, (n_pool, PAGE_SIZE, KV_LORA), 1.0),
        'cache_kpe': nrm(5, (n_pool, PAGE_SIZE, ROPE_DIM), 1.0),
        'page_table': page_table,
        'p_prompt': nrm(7, (DEPTH, BATCH, SEQ, PLE_DIM), 1.0),
        'p_sample': nrm(8, (DEPTH, DEC_BATCH, DEC_SEQ, PLE_DIM), 1.0),
        'attn_norm': gain(9, (DEPTH, D_MODEL)),
        'ffn_norm': gain(10, (DEPTH, D_MODEL)),
        'ple_norm': gain(11, (DEPTH, D_MODEL)),
        'final_norm': gain(12, (D_MODEL,)),
        'hg_wq': nrm(13, (N_A_LAYERS, D_MODEL, hw), D_MODEL ** -0.5),
        'hg_wf': nrm(14, (N_A_LAYERS, D_MODEL, hw), D_MODEL ** -0.5),
        'hg_wi': nrm(15, (N_A_LAYERS, D_MODEL, HG_HEADS * HG_DV), D_MODEL ** -0.5),
        'hg_wg': nrm(16, (N_A_LAYERS, D_MODEL, HG_HEADS * HG_DV), D_MODEL ** -0.5),
        'hg_onorm': gain(17, (N_A_LAYERS, HG_HEADS * HG_DV)),
        'hg_wo': nrm(18, (N_A_LAYERS, HG_HEADS * HG_DV, D_MODEL), (HG_HEADS * HG_DV) ** -0.5),
        'hg_lb_raw': nrm(19, (N_A_LAYERS + 1, hw), 0.1),
        'kv_norm': gain(20, (D_MODEL,)),
        'w_dkv': nrm(21, (D_MODEL, KV_LORA + ROPE_DIM), D_MODEL ** -0.5),
        'ckv_norm': gain(22, (KV_LORA,)),
        'w_uk': nrm(23, (KV_LORA, MLA_HEADS, NOPE_DIM), KV_LORA ** -0.5),
        'w_uv': nrm(24, (KV_LORA, MLA_HEADS, V_DIM), KV_LORA ** -0.5),
        'q_wd': nrm(25, (N_B_LAYERS, D_MODEL, Q_LORA), D_MODEL ** -0.5),
        'q_norm': gain(26, (N_B_LAYERS, Q_LORA)),
        'q_wu': nrm(27, (N_B_LAYERS, Q_LORA, MLA_HEADS * (NOPE_DIM + ROPE_DIM)), Q_LORA ** -0.5),
        'mla_wo': nrm(28, (N_B_LAYERS, MLA_HEADS * V_DIM, D_MODEL), (MLA_HEADS * V_DIM) ** -0.5),
        'ffn_wgate': nrm(29, (DEPTH, D_MODEL, D_FF), D_MODEL ** -0.5),
        'ffn_wup': nrm(30, (DEPTH, D_MODEL, D_FF), D_MODEL ** -0.5),
        'ffn_conv_w': nrm(31, (DEPTH, CONV_W, D_FF), CONV_W ** -0.5),
        'ffn_conv_b': nrm(32, (DEPTH, D_FF), 0.02),
        'ffn_wdown': nrm(33, (DEPTH, D_FF, D_MODEL), D_FF ** -0.5),
        'ple_wg': nrm(34, (DEPTH, D_MODEL, D_MODEL), D_MODEL ** -0.5),
        'ple_we': nrm(35, (DEPTH, PLE_DIM, D_MODEL), PLE_DIM ** -0.5),
    }


def reference(x_prompt, x_sample, state_hgrn, state_conv, cache_ckv, cache_kpe, page_table,
              p_prompt, p_sample,
              attn_norm, ffn_norm, ple_norm, final_norm,
              hg_wq, hg_wf, hg_wi, hg_wg, hg_onorm, hg_wo, hg_lb_raw,
              kv_norm, w_dkv, ckv_norm, w_uk, w_uv,
              q_wd, q_norm, q_wu, mla_wo,
              ffn_wgate, ffn_wup, ffn_conv_w, ffn_conv_b, ffn_wdown,
              ple_wg, ple_we):
    lb = jnp.cumsum(jax.nn.softmax(hg_lb_raw.astype(F32), axis=0), axis=0)[:N_A_LAYERS]

    def trunk(h, p, pos, s0_all, buf_all, attend):
        bsz, t, _ = h.shape
        states, bufs = [], []
        c = kpe = None
        for layer in range(DEPTH):
            xn = rmsnorm(h, attn_norm[layer])
            if layer < N_A_LAYERS:
                o, s_new = hgrn2_mixer(xn, hg_wq[layer], hg_wf[layer], hg_wi[layer], hg_wg[layer],
                                       hg_onorm[layer], hg_wo[layer], lb[layer], s0_all[layer])
                states.append(s_new)
            else:
                j = layer - N_A_LAYERS
                q_nope, q_pe = mla_queries(xn, q_wd[j], q_norm[j], q_wu[j], pos)
                o = attend(q_nope, q_pe, c, kpe).reshape(bsz, t, MLA_HEADS * V_DIM) @ mla_wo[j]
            h = h + o
            y, buf = conv_ffn(rmsnorm(h, ffn_norm[layer]), ffn_wgate[layer], ffn_wup[layer],
                              ffn_conv_w[layer], ffn_conv_b[layer], ffn_wdown[layer], buf_all[layer])
            bufs.append(buf)
            h = h + y
            h = ple_add(h, p[layer], ple_norm[layer], ple_wg[layer], ple_we[layer])
            if layer == N_A_LAYERS - 1:
                c, kpe = mla_kv_stream(h, kv_norm, w_dkv, ckv_norm, pos)
        return rmsnorm(h, final_norm), jnp.stack(states), jnp.stack(bufs), c, kpe

    b_p, t_p = x_prompt.shape[0], x_prompt.shape[1]
    t_s = x_sample.shape[1]
    pos_p = jnp.arange(t_p, dtype=jnp.int32)
    pos_s = PAST_LEN + jnp.arange(t_s, dtype=jnp.int32)
    s0_p = jnp.zeros((N_A_LAYERS, b_p, HG_HEADS, HG_DK, HG_DV), state_hgrn.dtype)
    buf0_p = jnp.zeros((DEPTH, b_p, CONV_W - 1, D_FF), state_conv.dtype)

    def attend_prompt(qn, qp, c, k):
        return mla_prompt_attention(qn, qp, c, k, w_uk, w_uv)

    def attend_sample(qn, qp, c, k):
        return mla_sample_attention(qn, qp, c, k, cache_ckv, cache_kpe, page_table, w_uk, w_uv)

    y_prompt, hgrn_p, conv_p, ckv_p, kpe_p = trunk(x_prompt, p_prompt, pos_p, s0_p, buf0_p, attend_prompt)
    y_sample, hgrn_s, conv_s, ckv_s, kpe_s = trunk(x_sample, p_sample, pos_s, state_hgrn, state_conv, attend_sample)
    return (y_prompt, y_sample, hgrn_p, hgrn_s, conv_p, conv_s, ckv_p, ckv_s, kpe_p, kpe_s)
```

```python
import functools
import math

import jax
import jax.numpy as jnp
from jax import lax
from jax.experimental import pallas as pl
from jax.experimental.pallas import tpu as pltpu

F32 = jnp.float32
BF16 = jnp.bfloat16
EPS = 1e-6
ROPE_THETA = 10000.0
HG_DK = 128
HG_CHUNK = 32
SUBLANES = 8
LANES = 128
VMEM_LIMIT = 56 * 1024 * 1024
VMEM_BUDGET = 46 * 1024 * 1024
NEW_KEY_ROWS = 16
PAGES_PER_STEP = 8


def _params(sem):
    return pltpu.CompilerParams(dimension_semantics=sem, vmem_limit_bytes=VMEM_LIMIT)


def _cast_rows(w_ref, wbf_ref):
    k = w_ref.shape[0]
    ch = math.gcd(k, 512)

    def step(c, carry):
        r = pl.multiple_of(c * ch, ch)
        wbf_ref[pl.ds(r, ch), :] = w_ref[pl.ds(r, ch), :].astype(BF16)
        return carry

    lax.fori_loop(0, k // ch, step, 0)


def _dot(a, b):
    return jnp.dot(a, b, preferred_element_type=F32)


def _dot_nt(a, b):
    return lax.dot_general(a, b, (((1,), (1,)), ((), ())), preferred_element_type=F32)


def _rmsnorm_body(h_ref, g_ref, o_ref):
    x = h_ref[...]
    inv = lax.rsqrt(jnp.mean(x * x, axis=-1, keepdims=True) + EPS)
    o_ref[...] = (x * inv * g_ref[...]).astype(o_ref.dtype)


def _rmsnorm(h, g, out_dtype, name):
    m, d = h.shape
    tr = math.gcd(m, 256)
    return pl.pallas_call(
        _rmsnorm_body,
        grid=(m // tr,),
        in_specs=[pl.BlockSpec((tr, d), lambda i: (i, 0)), pl.BlockSpec((1, d), lambda i: (0, 0))],
        out_specs=pl.BlockSpec((tr, d), lambda i: (i, 0)),
        out_shape=jax.ShapeDtypeStruct((m, d), out_dtype),
        compiler_params=_params(("arbitrary",)),
        name=name,
    )(h, g.reshape(1, d).astype(F32))


def _fused_matmul(name, xs, ws, w_x, row_ins, col_ins, epilogue, outs, tm, n_j):
    m = xs[0].shape[0]
    n_i = m // tm
    nx, nw, nr, nc, no = len(xs), len(ws), len(row_ins), len(col_ins), len(outs)

    def body(*refs):
        x_refs = refs[:nx]
        w_refs = refs[nx:nx + nw]
        r_refs = refs[nx + nw:nx + nw + nr]
        c_refs = refs[nx + nw + nr:nx + nw + nr + nc]
        o_refs = refs[nx + nw + nr + nc:nx + nw + nr + nc + no]
        wbf_refs = refs[nx + nw + nr + nc + no:]

        @pl.when(pl.program_id(1) == 0)
        def _():
            for w_ref, wbf in zip(w_refs, wbf_refs):
                _cast_rows(w_ref, wbf)

        xv = [x[...].astype(BF16) for x in x_refs]
        accs = [_dot(xv[w_x[j]], wbf_refs[j][...]) for j in range(nw)]
        res = epilogue(accs, [r[...] for r in r_refs], [c[...] for c in c_refs])
        for o_ref, r in zip(o_refs, res):
            o_ref[...] = r.astype(o_ref.dtype)

    in_specs = [pl.BlockSpec((tm, x.shape[1]), lambda j, i: (i, 0)) for x in xs]
    in_specs += [pl.BlockSpec((w.shape[0], w.shape[1] // n_j), lambda j, i: (0, j)) for w in ws]
    for arr, tiled in row_ins:
        if tiled:
            in_specs.append(pl.BlockSpec((tm, arr.shape[1] // n_j), lambda j, i: (i, j)))
        else:
            in_specs.append(pl.BlockSpec((tm, arr.shape[1]), lambda j, i: (i, 0)))
    in_specs += [pl.BlockSpec((c.shape[0], c.shape[1] // n_j), lambda j, i: (0, j)) for c in col_ins]
    out_specs = [pl.BlockSpec((tm, w // n_j), lambda j, i: (i, j)) for _, w in outs]
    out_shape = [jax.ShapeDtypeStruct((m, w), dt) for dt, w in outs]
    scratch = [pltpu.VMEM((w.shape[0], w.shape[1] // n_j), BF16) for w in ws]
    res = pl.pallas_call(
        body,
        grid=(n_j, n_i),
        in_specs=in_specs,
        out_specs=out_specs,
        out_shape=out_shape,
        scratch_shapes=scratch,
        compiler_params=_params(("arbitrary", "arbitrary")),
        name=name,
    )(*xs, *ws, *[a for a, _ in row_ins], *col_ins)
    return res


def _pick_n_tiles(widths, ks, x_row_bytes, tm, io_col_bytes, align=LANES):
    for n in range(1, max(widths) + 1):
        if any(w % n or (w // n) % align for w in widths):
            continue
        tns = [w // n for w in widths]
        wbytes = sum(k * t * (2 * 4 + 2) for k, t in zip(ks, tns))
        xbytes = 2 * tm * x_row_bytes
        iobytes = 2 * tm * max(tns) * io_col_bytes
        if wbytes + xbytes + iobytes <= VMEM_BUDGET and max(tns) <= 1024:
            return n
    raise ValueError("no column tiling fits the VMEM budget")


def _ffn_up(name, xn, wg, wu, conv_w, conv_b, p1, p2, n_prompt_tiles, tiles_per_seq, dec_seq, tm, n_j):
    m, d = xn.shape
    dff = wg.shape[1]
    tn = dff // n_j
    n_i = m // tm
    carry = SUBLANES

    def body(x_ref, wg_ref, wu_ref, cw_ref, cb_ref, p1_ref, p2_ref, hid_ref, tail_ref, asamp_ref,
             wg_bf, wu_bf, ext):
        i = pl.program_id(1)

        @pl.when(i == 0)
        def _():
            _cast_rows(wg_ref, wg_bf)
            _cast_rows(wu_ref, wu_bf)

        x = x_ref[...]
        a = _dot(x, wg_bf[...])
        u = _dot(x, wu_bf[...])
        is_sample = i >= n_prompt_tiles

        @pl.when(jnp.logical_or(i % tiles_per_seq == 0, is_sample))
        def _():
            ext[0:carry, :] = jnp.zeros((carry, tn), F32)

        ext[carry:carry + tm, :] = a
        s1 = ext[carry - 1:carry - 1 + tm, :]
        s2 = ext[carry - 2:carry - 2 + tm, :]
        cw = cw_ref[...]
        cb = cb_ref[...]

        def finish(prev1, prev2):
            conv = cb + prev2 * cw[0:1, :] + prev1 * cw[1:2, :] + a * cw[2:3, :]
            gelu = 0.5 * conv * (1.0 + lax.erf(conv * math.sqrt(0.5)))
            hid_ref[...] = (gelu * u).astype(hid_ref.dtype)

        @pl.when(jnp.logical_not(is_sample))
        def _():
            finish(s1, s2)

        @pl.when(is_sample)
        def _():
            t = lax.broadcasted_iota(jnp.int32, (tm, tn), 0) % dec_seq
            finish(jnp.where(t == 0, p1_ref[...], s1), jnp.where(t <= 1, p2_ref[...], s2))
            asamp_ref[...] = a

        tail = a[tm - carry:tm, :]
        ext[0:carry, :] = tail
        tail_ref[0] = tail

    return pl.pallas_call(
        body,
        grid=(n_j, n_i),
        in_specs=[
            pl.BlockSpec((tm, d), lambda j, i: (i, 0)),
            pl.BlockSpec((d, tn), lambda j, i: (0, j)),
            pl.BlockSpec((d, tn), lambda j, i: (0, j)),
            pl.BlockSpec((conv_w.shape[0], tn), lambda j, i: (0, j)),
            pl.BlockSpec((1, tn), lambda j, i: (0, j)),
            pl.BlockSpec((tm, tn), lambda j, i: (0, j)),
            pl.BlockSpec((tm, tn), lambda j, i: (0, j)),
        ],
        out_specs=[
            pl.BlockSpec((tm, tn), lambda j, i: (i, j)),
            pl.BlockSpec((1, carry, tn), lambda j, i: (i, 0, j)),
            pl.BlockSpec((tm, tn), lambda j, i: (0, j)),
        ],
        out_shape=[
            jax.ShapeDtypeStruct((m, dff), BF16),
            jax.ShapeDtypeStruct((n_i, carry, dff), F32),
            jax.ShapeDtypeStruct((tm, dff), F32),
        ],
        scratch_shapes=[pltpu.VMEM((d, tn), BF16), pltpu.VMEM((d, tn), BF16),
                        pltpu.VMEM((tm + carry, tn), F32)],
        compiler_params=_params(("arbitrary", "arbitrary")),
        name=name,
    )(xn, wg, wu, conv_w, conv_b.reshape(1, dff), p1, p2)


def _gla(name, q, f, v, gate, onorm, s0, n_seq, seq_len, chunk, tb, hb):
    hw = q.shape[1]
    n_heads = hw // HG_DK
    nt = seq_len // tb
    n_hg = n_heads // hb
    dk = HG_DK

    def body(q_ref, f_ref, v_ref, g_ref, on_ref, s0_ref, o_ref, sf_ref, st_scr):
        t = pl.program_id(2)

        @pl.when(t == 0)
        def _():
            for hd in range(hb):
                st_scr[hd] = s0_ref[0, hd].T

        row = lax.broadcasted_iota(jnp.int32, (chunk, chunk), 0)
        col = lax.broadcasted_iota(jnp.int32, (chunk, chunk), 1)
        tril = row >= col
        tril_f = tril.astype(F32)

        def chunk_step(c, carry):
            r = pl.multiple_of(c * chunk, chunk)
            for hd in range(hb):
                ls = slice(hd * dk, (hd + 1) * dk)
                qc = q_ref[pl.ds(r, chunk), ls]
                fc = f_ref[pl.ds(r, chunk), ls]
                vc = v_ref[pl.ds(r, chunk), ls].astype(BF16)
                b = jnp.dot(tril_f, jnp.log(fc), precision=lax.Precision.HIGHEST,
                            preferred_element_type=F32)
                b_end = b[chunk - 1:chunk, :]
                kc = 1.0 - fc
                q_dec = (qc * jnp.exp(b)).astype(BF16)
                k_inc = (kc * jnp.exp(-b)).astype(BF16)
                k_dec = (kc * jnp.exp(b_end - b)).astype(BF16)
                st = st_scr[hd]
                a = jnp.where(tril, _dot_nt(q_dec, k_inc), 0.0)
                o = _dot_nt(q_dec, st.astype(BF16)) + _dot(a.astype(BF16), vc)
                upd = lax.dot_general(vc, k_dec, (((0,), (0,)), ((), ())), preferred_element_type=F32)
                st_scr[hd] = st * jnp.exp(b_end) + upd
                y = o * lax.rsqrt(jnp.mean(o * o, axis=-1, keepdims=True) + EPS) * on_ref[:, ls]
                o_ref[pl.ds(r, chunk), ls] = (y * g_ref[pl.ds(r, chunk), ls]).astype(o_ref.dtype)
            return carry

        lax.fori_loop(0, tb // chunk, chunk_step, 0)

        @pl.when(t == nt - 1)
        def _():
            for hd in range(hb):
                sf_ref[0, hd] = st_scr[hd].T

    row_spec = pl.BlockSpec((tb, hb * dk), lambda s, h, t: (s * nt + t, h))
    return pl.pallas_call(
        body,
        grid=(n_seq, n_hg, nt),
        in_specs=[row_spec, row_spec, row_spec, row_spec,
                  pl.BlockSpec((1, hb * dk), lambda s, h, t: (0, h)),
                  pl.BlockSpec((1, hb, dk, dk), lambda s, h, t: (s, h, 0, 0))],
        out_specs=[row_spec, pl.BlockSpec((1, hb, dk, dk), lambda s, h, t: (s, h, 0, 0))],
        out_shape=[jax.ShapeDtypeStruct((n_seq * seq_len, hw), BF16),
                   jax.ShapeDtypeStruct((n_seq, n_heads, dk, dk), F32)],
        scratch_shapes=[pltpu.VMEM((hb, dk, dk), F32)],
        compiler_params=_params(("arbitrary", "arbitrary", "arbitrary")),
        name=name,
    )(q, f, v, gate, onorm.reshape(1, hw).astype(F32), s0)


def _flash_prompt(name, q, k, v, n_b, seq, n_heads, qk_w, v_w, blk):
    nq = seq // blk

    def body(q_ref, k_ref, v_ref, o_ref, m_scr, l_scr, acc_scr):
        qi = pl.program_id(2)
        ki = pl.program_id(3)

        @pl.when(ki == 0)
        def _():
            m_scr[...] = jnp.full(m_scr.shape, -jnp.inf, F32)
            l_scr[...] = jnp.zeros(l_scr.shape, F32)
            acc_scr[...] = jnp.zeros(acc_scr.shape, F32)

        @pl.when(ki <= qi)
        def _():
            s = _dot_nt(q_ref[...], k_ref[...])
            row = lax.broadcasted_iota(jnp.int32, (blk, blk), 0)
            col = lax.broadcasted_iota(jnp.int32, (blk, blk), 1)
            s = jnp.where(jnp.logical_or(ki < qi, col <= row), s, -jnp.inf)
            m_prev = m_scr[...]
            m_new = jnp.maximum(m_prev, jnp.max(s, axis=-1, keepdims=True))
            alpha = jnp.exp(m_prev - m_new)
            p = jnp.exp(s - m_new)
            l_scr[...] = alpha * l_scr[...] + jnp.sum(p, axis=-1, keepdims=True)
            acc_scr[...] = alpha * acc_scr[...] + _dot(p.astype(BF16), v_ref[...])
            m_scr[...] = m_new

        @pl.when(ki == nq - 1)
        def _():
            o_ref[...] = (acc_scr[...] / l_scr[...]).astype(o_ref.dtype)

    return pl.pallas_call(
        body,
        grid=(n_b, n_heads, nq, nq),
        in_specs=[
            pl.BlockSpec((blk, qk_w), lambda b, h, qi, ki: (b * nq + qi, h)),
            pl.BlockSpec((blk, qk_w), lambda b, h, qi, ki: (b * nq + jnp.minimum(ki, qi), h)),
            pl.BlockSpec((blk, v_w), lambda b, h, qi, ki: (b * nq + jnp.minimum(ki, qi), h)),
        ],
        out_specs=pl.BlockSpec((blk, v_w), lambda b, h, qi, ki: (b * nq + qi, h)),
        out_shape=jax.ShapeDtypeStruct((n_b * seq, n_heads * v_w), BF16),
        scratch_shapes=[pltpu.VMEM((blk, 1), F32), pltpu.VMEM((blk, 1), F32), pltpu.VMEM((blk, v_w), F32)],
        compiler_params=_params(("arbitrary", "arbitrary", "arbitrary", "arbitrary")),
        name=name,
    )(q, k, v)


def _decode_attention(name, ql, qp, c_new, k_new, cache_ckv, cache_kpe, page_table, dec_seq):
    n_b, n_r, c_w = ql.shape
    p_w = qp.shape[2]
    page = cache_ckv.shape[1]
    n_pages = page_table.shape[1]
    g = math.gcd(n_pages, PAGES_PER_STEP)
    n_steps = n_pages // g
    n_new = c_new.shape[1]

    def body(pt_ref, ql_ref, qp_ref, cn_ref, kn_ref, *rest):
        c_refs = rest[:g]
        k_refs = rest[g:2 * g]
        o_ref, m_scr, l_scr, acc_scr = rest[2 * g:]
        s_i = pl.program_id(1)

        @pl.when(s_i == 0)
        def _():
            m_scr[...] = jnp.full(m_scr.shape, -jnp.inf, F32)
            l_scr[...] = jnp.zeros(l_scr.shape, F32)
            acc_scr[...] = jnp.zeros(acc_scr.shape, F32)

        q_l = ql_ref[0]
        q_p = qp_ref[0]

        def update(s, c_bf):
            m_prev = m_scr[...]
            m_new = jnp.maximum(m_prev, jnp.max(s, axis=-1, keepdims=True))
            alpha = jnp.exp(m_prev - m_new)
            p = jnp.exp(s - m_new)
            l_scr[...] = alpha * l_scr[...] + jnp.sum(p, axis=-1, keepdims=True)
            acc_scr[...] = alpha * acc_scr[...] + _dot(p.astype(BF16), c_bf)
            m_scr[...] = m_new

        c_bf = jnp.concatenate([c[0].astype(BF16) for c in c_refs], axis=0)
        k_bf = jnp.concatenate([k[0].astype(BF16) for k in k_refs], axis=0)
        update(_dot_nt(q_l, c_bf) + _dot_nt(q_p, k_bf), c_bf)

        @pl.when(s_i == n_steps - 1)
        def _():
            cn = cn_ref[0].astype(BF16)
            kn = kn_ref[0].astype(BF16)
            s = _dot_nt(q_l, cn) + _dot_nt(q_p, kn)
            tok = lax.broadcasted_iota(jnp.int32, (n_r, n_new), 0) % dec_seq
            key = lax.broadcasted_iota(jnp.int32, (n_r, n_new), 1)
            s = jnp.where(key <= tok, s, -jnp.inf)
            update(s, cn)
            o_ref[0] = acc_scr[...] / l_scr[...]

    def cache_spec(w, idx):
        return pl.BlockSpec((1, page, w), lambda b, s, pt: (pt[b, s * g + idx], 0, 0))

    in_specs = [
        pl.BlockSpec((1, n_r, c_w), lambda b, s, pt: (b, 0, 0)),
        pl.BlockSpec((1, n_r, p_w), lambda b, s, pt: (b, 0, 0)),
        pl.BlockSpec((1, n_new, c_w), lambda b, s, pt: (b, 0, 0)),
        pl.BlockSpec((1, n_new, p_w), lambda b, s, pt: (b, 0, 0)),
    ]
    in_specs += [cache_spec(c_w, idx) for idx in range(g)]
    in_specs += [cache_spec(p_w, idx) for idx in range(g)]
    grid_spec = pltpu.PrefetchScalarGridSpec(
        num_scalar_prefetch=1,
        grid=(n_b, n_steps),
        in_specs=in_specs,
        out_specs=pl.BlockSpec((1, n_r, c_w), lambda b, s, pt: (b, 0, 0)),
        scratch_shapes=[pltpu.VMEM((n_r, 1), F32), pltpu.VMEM((n_r, 1), F32), pltpu.VMEM((n_r, c_w), F32)],
    )
    return pl.pallas_call(
        body,
        grid_spec=grid_spec,
        out_shape=jax.ShapeDtypeStruct((n_b, n_r, c_w), F32),
        compiler_params=_params(("arbitrary", "arbitrary")),
        name=name,
    )(page_table, ql, qp, c_new, k_new, *([cache_ckv] * g), *([cache_kpe] * g))


def _head_matmul(name, x, w, out_dtype):
    n_h, m, k = x.shape
    n = w.shape[2]

    def body(x_ref, w_ref, o_ref):
        o_ref[0] = _dot(x_ref[0].astype(BF16), w_ref[0].astype(BF16)).astype(o_ref.dtype)

    return pl.pallas_call(
        body,
        grid=(n_h,),
        in_specs=[pl.BlockSpec((1, m, k), lambda h: (h, 0, 0)), pl.BlockSpec((1, k, n), lambda h: (h, 0, 0))],
        out_specs=pl.BlockSpec((1, m, n), lambda h: (h, 0, 0)),
        out_shape=jax.ShapeDtypeStruct((n_h, m, n), out_dtype),
        compiler_params=_params(("arbitrary",)),
        name=name,
    )(x, w)


def kernel(x_prompt, x_sample, state_hgrn, state_conv, cache_ckv, cache_kpe, page_table, p_prompt, p_sample, attn_norm, ffn_norm, ple_norm, final_norm, hg_wq, hg_wf, hg_wi, hg_wg, hg_onorm, hg_wo, hg_lb_raw, kv_norm, w_dkv, ckv_norm, w_uk, w_uv, q_wd, q_norm, q_wu, mla_wo, ffn_wgate, ffn_wup, ffn_conv_w, ffn_conv_b, ffn_wdown, ple_wg, ple_we):
    n_b, seq, d = x_prompt.shape
    n_bd, dec_seq, _ = x_sample.shape
    depth = attn_norm.shape[0]
    n_a = hg_wq.shape[0]
    dff = ffn_wgate.shape[2]
    ple_dim = p_prompt.shape[3]
    kv_lora, mla_heads, nope = w_uk.shape
    v_dim = w_uv.shape[2]
    rope_dim = w_dkv.shape[1] - kv_lora
    half = rope_dim // 2
    q_lora = q_wd.shape[2]
    page = cache_ckv.shape[1]
    past_len = page_table.shape[1] * page
    hg_heads = hg_wq.shape[2] // HG_DK
    scale = float(nope + rope_dim) ** -0.5
    qk_w = 2 * LANES
    assert nope == LANES and v_dim == LANES and rope_dim <= LANES and rope_dim % 2 == 0

    mp, ms = n_b * seq, n_bd * dec_seq
    m = mp + ms
    tm = ms
    assert tm % SUBLANES == 0 and seq % tm == 0 and dec_seq >= 2 and dec_seq <= SUBLANES
    n_prompt_tiles = mp // tm
    tiles_per_seq = seq // tm

    h = jnp.concatenate([x_prompt.reshape(mp, d), x_sample.reshape(ms, d)], axis=0)
    p_all = jnp.concatenate([p_prompt.reshape(depth, mp, ple_dim), p_sample.reshape(depth, ms, ple_dim)], axis=1)

    pos = jnp.concatenate([jnp.tile(jnp.arange(seq, dtype=jnp.int32), n_b),
                           jnp.tile(past_len + jnp.arange(dec_seq, dtype=jnp.int32), n_bd)])
    inv = ROPE_THETA ** (-jnp.arange(half, dtype=F32) / half)
    ang = pos.astype(F32)[:, None] * inv[None, :]
    cos2 = jnp.concatenate([jnp.cos(ang), jnp.cos(ang)], axis=1)
    sin2 = jnp.concatenate([-jnp.sin(ang), jnp.sin(ang)], axis=1)
    cos_pad = jnp.pad(cos2, ((0, 0), (0, LANES - rope_dim)))
    sin_pad = jnp.pad(sin2, ((0, 0), (0, LANES - rope_dim)))

    states_p, states_s, conv_p, conv_s = [], [], [], []
    c_all = kpe_all = None

    for layer in range(depth):
        xn = _rmsnorm(h, attn_norm[layer], BF16, f"attn_norm_{layer}")
        if layer < n_a:
            hw = hg_heads * HG_DK
            n_slots = hg_lb_raw.shape[0]

            def qf_epilogue(accs, rows, cols, layer=layer):
                raw = cols[0]
                e = jnp.exp(raw - jnp.max(raw, axis=0, keepdims=True))
                lb = jnp.sum(e[:layer + 1], axis=0, keepdims=True) / jnp.sum(e, axis=0, keepdims=True)
                return jax.nn.silu(accs[0]), lb + (1.0 - lb) * jax.nn.sigmoid(accs[1])

            n_j = _pick_n_tiles([hw, hw], [d, d], d * 2, tm, 2 * 4)
            q_act, f_act = _fused_matmul(
                f"hgrn_qf_{layer}", [xn], [hg_wq[layer], hg_wf[layer]], [0, 0], [], [hg_lb_raw.astype(F32)],
                qf_epilogue, [(F32, hw), (F32, hw)], tm, n_j)
            v_act, g_act = _fused_matmul(
                f"hgrn_ig_{layer}", [xn], [hg_wi[layer], hg_wg[layer]], [0, 0], [], [],
                lambda accs, rows, cols: (accs[0], jax.nn.silu(accs[1])), [(F32, hw), (F32, hw)], tm, n_j)

            hb = math.gcd(hg_heads, 4)
            tb = math.gcd(seq, 256)
            o_p, s_p = _gla(f"gla_prompt_{layer}", q_act, f_act, v_act, g_act, hg_onorm[layer],
                            jnp.zeros((n_b, hg_heads, HG_DK, HG_DK), F32), n_b, seq, HG_CHUNK, tb, hb)

            def pad_seq(a, fill):
                a = a[mp:].reshape(n_bd, dec_seq, hw)
                a = jnp.pad(a, ((0, 0), (0, SUBLANES - dec_seq), (0, 0)), constant_values=fill)
                return a.reshape(n_bd * SUBLANES, hw)

            o_s, s_s = _gla(f"gla_sample_{layer}", pad_seq(q_act, 0.0), pad_seq(f_act, 1.0), pad_seq(v_act, 0.0),
                            pad_seq(g_act, 0.0), hg_onorm[layer], state_hgrn[layer].astype(F32),
                            n_bd, SUBLANES, SUBLANES, SUBLANES, math.gcd(hg_heads, 8))
            o_s = o_s.reshape(n_bd, SUBLANES, hw)[:, :dec_seq].reshape(ms, hw)
            o_all = jnp.concatenate([o_p, o_s], axis=0)
            states_p.append(s_p)
            states_s.append(s_s.astype(state_hgrn.dtype))
            w_out = hg_wo[layer]
        else:
            j = layer - n_a
            tmq = math.gcd(tm, 256)
            n_jq = _pick_n_tiles([q_lora], [d], d * 2, tmq, 2, align=q_lora)

            def cq_epilogue(accs, rows, cols):
                x = accs[0]
                return (x * lax.rsqrt(jnp.mean(x * x, axis=-1, keepdims=True) + EPS) * cols[0],)

            cq = _fused_matmul(f"mla_qdown_{j}", [xn], [q_wd[j]], [0], [], [q_norm[j].reshape(1, q_lora).astype(F32)],
                               cq_epilogue, [(BF16, q_lora)], tmq, n_jq)[0]

            wq3 = q_wu[j].reshape(q_lora, mla_heads, nope + rope_dim)
            wq_pad = jnp.pad(wq3, ((0, 0), (0, 0), (0, qk_w - nope - rope_dim))).reshape(q_lora, mla_heads * qk_w)
            n_jh = _pick_n_tiles([mla_heads * qk_w], [q_lora], q_lora * 2, tm, 2, align=qk_w)
            heads_per_tile = mla_heads // n_jh

            def q_epilogue(accs, rows, cols):
                cos_t, sin_t = rows
                lane = lax.broadcasted_iota(jnp.int32, cos_t.shape, 1)
                outs = []
                for hh in range(heads_per_tile):
                    lo = accs[0][:, hh * qk_w:hh * qk_w + LANES]
                    hi = accs[0][:, hh * qk_w + LANES:(hh + 1) * qk_w]
                    swapped = jnp.where(lane < half, pltpu.roll(hi, LANES - half, 1), pltpu.roll(hi, half, 1))
                    outs += [lo * scale, (hi * cos_t + swapped * sin_t) * scale]
                return (jnp.concatenate(outs, axis=1),)

            qfull = _fused_matmul(f"mla_qup_{j}", [cq], [wq_pad], [0], [(cos_pad, False), (sin_pad, False)], [],
                                  q_epilogue, [(BF16, mla_heads * qk_w)], tm, n_jh)[0]

            kpe_pad = jnp.pad(kpe_all[:mp], ((0, 0), (0, LANES - rope_dim)))
            n_jk = mla_heads // 2 if mla_heads % 2 == 0 else mla_heads
            heads_k = mla_heads // n_jk

            def kv_epilogue(accs, rows, cols):
                ks = []
                for hh in range(heads_k):
                    ks += [accs[0][:, hh * nope:(hh + 1) * nope], rows[0]]
                return jnp.concatenate(ks, axis=1), accs[1]

            tmk = math.gcd(mp, 512)
            kfull, vfull = _fused_matmul(
                f"mla_kv_up_{j}", [c_all[:mp]], [w_uk.reshape(kv_lora, mla_heads * nope), w_uv.reshape(kv_lora, mla_heads * v_dim)],
                [0, 0], [(kpe_pad, False)], [], kv_epilogue,
                [(BF16, mla_heads * qk_w), (BF16, mla_heads * v_dim)], tmk, n_jk)
            blk = math.gcd(seq, 256)
            o_p = _flash_prompt(f"mla_prompt_attn_{j}", qfull, kfull, vfull, n_b, seq, mla_heads, qk_w, v_dim, blk)

            q_s = qfull[mp:].reshape(n_bd, dec_seq, mla_heads, qk_w)
            q_nope_h = q_s[..., :nope].transpose(2, 0, 1, 3).reshape(mla_heads, ms, nope)
            q_lat = _head_matmul(f"mla_q_absorb_{j}", q_nope_h, w_uk.transpose(1, 2, 0), BF16)
            ql = q_lat.reshape(mla_heads, n_bd, dec_seq, kv_lora).transpose(1, 0, 2, 3).reshape(n_bd, mla_heads * dec_seq, kv_lora)
            qp = q_s[..., nope:nope + rope_dim].transpose(0, 2, 1, 3).reshape(n_bd, mla_heads * dec_seq, rope_dim)
            pad_new = ((0, 0), (0, NEW_KEY_ROWS - dec_seq), (0, 0))
            c_new = jnp.pad(c_all[mp:].reshape(n_bd, dec_seq, kv_lora), pad_new)
            k_new = jnp.pad(kpe_all[mp:].reshape(n_bd, dec_seq, rope_dim), pad_new)
            lat = _decode_attention(f"mla_sample_attn_{j}", ql, qp, c_new, k_new, cache_ckv, cache_kpe,
                                    page_table.astype(jnp.int32), dec_seq)
            lat_h = lat.reshape(n_bd, mla_heads, dec_seq, kv_lora).transpose(1, 0, 2, 3).reshape(mla_heads, ms, kv_lora)
            o_sh = _head_matmul(f"mla_v_up_{j}", lat_h, w_uv.transpose(1, 0, 2), BF16)
            o_s = o_sh.transpose(1, 0, 2).reshape(ms, mla_heads * v_dim)
            o_all = jnp.concatenate([o_p, o_s], axis=0)
            w_out = mla_wo[j]

        n_jo = _pick_n_tiles([d], [w_out.shape[0]], w_out.shape[0] * 2, tm, 2 * 4)
        h = _fused_matmul(f"mixer_out_{layer}", [o_all], [w_out], [0], [(h, True)], [],
                          lambda accs, rows, cols: (rows[0] + accs[0],), [(F32, d)], tm, n_jo)[0]

        xn = _rmsnorm(h, ffn_norm[layer], BF16, f"ffn_norm_{layer}")
        buf = state_conv[layer].astype(F32)
        zeros = jnp.zeros((n_bd, dec_seq, dff), F32)
        p1 = zeros.at[:, 0].set(buf[:, 1]).reshape(ms, dff)
        p2 = zeros.at[:, 0].set(buf[:, 0]).at[:, 1].set(buf[:, 1]).reshape(ms, dff)
        n_jf = _pick_n_tiles([dff, dff], [d, d], d * 2, tm, 4 * 4)
        hid, tails, a_samp = _ffn_up(f"ffn_up_{layer}", xn, ffn_wgate[layer], ffn_wup[layer], ffn_conv_w[layer],
                                     ffn_conv_b[layer], p1, p2, n_prompt_tiles, tiles_per_seq, dec_seq, tm, n_jf)
        conv_p.append(tails[tiles_per_seq - 1:n_prompt_tiles:tiles_per_seq, SUBLANES - 2:, :].astype(state_conv.dtype))
        conv_s.append(a_samp.reshape(n_bd, dec_seq, dff)[:, dec_seq - 2:, :].astype(state_conv.dtype))
        tmd = math.gcd(tm, 256)
        n_jd = _pick_n_tiles([d], [dff], dff * 2, tmd, 2 * 4)
        h = _fused_matmul(f"ffn_down_{layer}", [hid], [ffn_wdown[layer]], [0], [(h, True)], [],
                          lambda accs, rows, cols: (rows[0] + accs[0],), [(F32, d)], tmd, n_jd)[0]

        xn = _rmsnorm(h, ple_norm[layer], BF16, f"ple_norm_{layer}")
        n_jp = _pick_n_tiles([d, d], [d, ple_dim], (d + 2 * ple_dim) * 2, tm, 2 * 4)
        h = _fused_matmul(f"ple_{layer}", [xn, p_all[layer]], [ple_wg[layer], ple_we[layer]], [0, 1], [(h, True)], [],
                          lambda accs, rows, cols: (rows[0] + jax.nn.sigmoid(accs[0]) * accs[1],),
                          [(F32, d)], tm, n_jp)[0]

        if layer == n_a - 1:
            xk = _rmsnorm(h, kv_norm, BF16, "kv_norm")
            swap = jnp.concatenate([jnp.arange(half, rope_dim), jnp.arange(half)])
            w_pe = w_dkv[:, kv_lora:]

            def kvs_epilogue(accs, rows, cols):
                x = accs[0]
                c = x * lax.rsqrt(jnp.mean(x * x, axis=-1, keepdims=True) + EPS) * cols[0]
                return c, accs[1] * rows[0] + accs[2] * rows[1]

            c_all, kpe_all = _fused_matmul(
                "kv_stream", [xk], [w_dkv[:, :kv_lora], w_pe, w_pe[:, swap]], [0, 0, 0],
                [(cos2, False), (sin2, False)], [ckv_norm.reshape(1, kv_lora).astype(F32)],
                kvs_epilogue, [(F32, kv_lora), (F32, rope_dim)], tm, 1)

    y = _rmsnorm(h, final_norm, x_prompt.dtype, "final_norm")
    return (y[:mp].reshape(n_b, seq, d), y[mp:].reshape(n_bd, dec_seq, d),
            jnp.stack(states_p).astype(state_hgrn.dtype), jnp.stack(states_s),
            jnp.stack(conv_p), jnp.stack(conv_s),
            c_all[:mp].reshape(n_b, seq, kv_lora), c_all[mp:].reshape(n_bd, dec_seq, kv_lora),
            kpe_all[:mp].reshape(n_b, seq, rope_dim), kpe_all[mp:].reshape(n_bd, dec_seq, rope_dim))
```

```python
import math

import jax
import jax.numpy as jnp
from jax import lax
from jax.experimental import pallas as pl
from jax.experimental.pallas import tpu as pltpu

F32 = jnp.float32
BF16 = jnp.bfloat16
EPS = 1e-6
ROPE_THETA = 10000.0
HG_DK = 128
HG_CHUNK = 32
SUBLANES = 8
LANES = 128
BF16_ROWS = 16
VMEM_LIMIT = 56 * 1024 * 1024
VMEM_BUDGET = 46 * 1024 * 1024
PAGES_PER_STEP = 16
DECODE_CHAINS = 2
DECODE_SLOTS = 3
MAX_ROW_TILE = 1152


def _params(sem):
    return pltpu.CompilerParams(dimension_semantics=sem, vmem_limit_bytes=VMEM_LIMIT)


def _cast_rows(w_ref, wbf_ref):
    k = w_ref.shape[0]
    ch = math.gcd(k, 512)

    def step(c, carry):
        r = pl.multiple_of(c * ch, ch)
        wbf_ref[pl.ds(r, ch), :] = w_ref[pl.ds(r, ch), :].astype(BF16)
        return carry

    lax.fori_loop(0, k // ch, step, 0)


def _dot(a, b):
    return jnp.dot(a, b, preferred_element_type=F32)


def _dot_nt(a, b):
    return lax.dot_general(a, b, (((1,), (1,)), ((), ())), preferred_element_type=F32)


def _dot_tn(a, b):
    return lax.dot_general(a, b, (((0,), (0,)), ((), ())), preferred_element_type=F32)


def _split3(x):
    p1 = x.astype(BF16).astype(F32)
    r1 = x - p1
    p2 = r1.astype(BF16).astype(F32)
    p3 = (r1 - p2).astype(BF16).astype(F32)
    return p1, p2, p3


def _weight_spec(w, n_j):
    arr, layer = w
    k, n = arr.shape[-2:]
    if arr.ndim == 3:
        return pl.BlockSpec((None, k, n // n_j), lambda j, i: (layer, 0, j))
    return pl.BlockSpec((k, n // n_j), lambda j, i: (0, j))


def _rmsnorm_body(h_ref, g_ref, o_ref):
    x = h_ref[...]
    inv = lax.rsqrt(jnp.mean(x * x, axis=-1, keepdims=True) + EPS)
    o_ref[...] = (x * inv * g_ref[...]).astype(o_ref.dtype)


def _rmsnorm(h, g, out_dtype, name):
    m, d = h.shape
    tr = math.gcd(m, 256)
    return pl.pallas_call(
        _rmsnorm_body,
        grid=(m // tr,),
        in_specs=[pl.BlockSpec((tr, d), lambda i: (i, 0)), pl.BlockSpec((1, d), lambda i: (0, 0))],
        out_specs=pl.BlockSpec((tr, d), lambda i: (i, 0)),
        out_shape=jax.ShapeDtypeStruct((m, d), out_dtype),
        compiler_params=_params(("arbitrary",)),
        name=name,
    )(h, g.reshape(1, d).astype(F32))


def _fused_matmul(name, xs, ws, w_x, row_ins, col_ins, epilogue, outs, tm, n_j, m_rows=None):
    m = xs[0].shape[0] if m_rows is None else m_rows
    n_i = m // tm
    nx, nw, nr, nc, no = len(xs), len(ws), len(row_ins), len(col_ins), len(outs)

    def body(*refs):
        x_refs = refs[:nx]
        w_refs = refs[nx:nx + nw]
        r_refs = refs[nx + nw:nx + nw + nr]
        c_refs = refs[nx + nw + nr:nx + nw + nr + nc]
        o_refs = refs[nx + nw + nr + nc:nx + nw + nr + nc + no]
        wbf_refs = refs[nx + nw + nr + nc + no:]

        @pl.when(pl.program_id(1) == 0)
        def _():
            for w_ref, wbf in zip(w_refs, wbf_refs):
                _cast_rows(w_ref, wbf)

        xv = [x[...].astype(BF16) for x in x_refs]
        accs = [_dot(xv[w_x[j]], wbf_refs[j][...]) for j in range(nw)]
        res = epilogue(accs, [r[...] for r in r_refs], [c[...] for c in c_refs])
        for o_ref, r in zip(o_refs, res):
            o_ref[...] = r.astype(o_ref.dtype)

    in_specs = [pl.BlockSpec((tm, x.shape[1]), lambda j, i: (i, 0)) for x in xs]
    in_specs += [_weight_spec(w, n_j) for w in ws]
    for arr, tiled in row_ins:
        if tiled:
            in_specs.append(pl.BlockSpec((tm, arr.shape[1] // n_j), lambda j, i: (i, j)))
        else:
            in_specs.append(pl.BlockSpec((tm, arr.shape[1]), lambda j, i: (i, 0)))
    in_specs += [pl.BlockSpec((c.shape[0], c.shape[1] // n_j), lambda j, i: (0, j)) for c in col_ins]
    out_specs = [pl.BlockSpec((tm, w // n_j), lambda j, i: (i, j)) for _, w in outs]
    out_shape = [jax.ShapeDtypeStruct((m, w), dt) for dt, w in outs]
    scratch = [pltpu.VMEM((w.shape[-2], w.shape[-1] // n_j), BF16) for w, _ in ws]
    return pl.pallas_call(
        body,
        grid=(n_j, n_i),
        in_specs=in_specs,
        out_specs=out_specs,
        out_shape=out_shape,
        scratch_shapes=scratch,
        compiler_params=_params(("arbitrary", "arbitrary")),
        name=name,
    )(*xs, *[w for w, _ in ws], *[a for a, _ in row_ins], *col_ins)


def _pick_n_tiles(widths, ks, x_row_bytes, tm, io_col_bytes, align=LANES):
    for n in range(1, max(widths) + 1):
        if any(w % n or (w // n) % align for w in widths):
            continue
        tns = [w // n for w in widths]
        wbytes = sum(k * t * (2 * 4 + 2) for k, t in zip(ks, tns))
        xbytes = 2 * tm * x_row_bytes
        iobytes = 2 * tm * max(tns) * io_col_bytes
        if wbytes + xbytes + iobytes <= VMEM_BUDGET and max(tns) <= 1024:
            return n
    raise ValueError("no column tiling fits the VMEM budget")


def _pick_row_tile(m):
    for tm in range(min(m, MAX_ROW_TILE), 0, -1):
        if m % tm == 0 and tm % BF16_ROWS == 0:
            return tm
    raise ValueError("row count has no bf16-tile-aligned divisor")


def _ffn_up(name, xn, wg, wu, conv_w, conv_b, e1, e2, n_prompt_tiles, tiles_per_seq, dec_seq, tm, n_j):
    m, d = xn.shape
    dff = wg[0].shape[-1]
    tn = dff // n_j
    n_i = m // tm
    carry = SUBLANES

    def body(x_ref, wg_ref, wu_ref, cw_ref, cb_ref, e1_ref, e2_ref, hid_ref, tail_ref, asamp_ref,
             wg_bf, wu_bf, ext):
        i = pl.program_id(1)

        @pl.when(i == 0)
        def _():
            _cast_rows(wg_ref, wg_bf)
            _cast_rows(wu_ref, wu_bf)

        x = x_ref[...]
        a = _dot(x, wg_bf[...])
        u = _dot(x, wu_bf[...])
        is_sample = i >= n_prompt_tiles

        @pl.when(jnp.logical_or(i % tiles_per_seq == 0, is_sample))
        def _():
            ext[0:carry, :] = jnp.zeros((carry, tn), F32)

        ext[carry:carry + tm, :] = a
        s1 = ext[carry - 1:carry - 1 + tm, :]
        s2 = ext[carry - 2:carry - 2 + tm, :]
        cw = cw_ref[...]
        cb = cb_ref[...]

        def finish(prev1, prev2):
            conv = cb + prev2 * cw[0:1, :] + prev1 * cw[1:2, :] + a * cw[2:3, :]
            gelu = 0.5 * conv * (1.0 + lax.erf(conv * math.sqrt(0.5)))
            hid_ref[...] = (gelu * u).astype(hid_ref.dtype)

        @pl.when(jnp.logical_not(is_sample))
        def _():
            finish(s1, s2)

        @pl.when(is_sample)
        def _():
            t = lax.broadcasted_iota(jnp.int32, (tm, tn), 0) % dec_seq
            finish(jnp.where(t == 0, e1_ref[...], s1), jnp.where(t <= 1, e2_ref[...], s2))
            asamp_ref[...] = a

        tail = a[tm - carry:tm, :]
        ext[0:carry, :] = tail
        tail_ref[0] = tail

    return pl.pallas_call(
        body,
        grid=(n_j, n_i),
        in_specs=[
            pl.BlockSpec((tm, d), lambda j, i: (i, 0)),
            _weight_spec(wg, n_j),
            _weight_spec(wu, n_j),
            pl.BlockSpec((conv_w.shape[0], tn), lambda j, i: (0, j)),
            pl.BlockSpec((1, tn), lambda j, i: (0, j)),
            pl.BlockSpec((tm, tn), lambda j, i: (0, j)),
            pl.BlockSpec((tm, tn), lambda j, i: (0, j)),
        ],
        out_specs=[
            pl.BlockSpec((tm, tn), lambda j, i: (i, j)),
            pl.BlockSpec((1, carry, tn), lambda j, i: (i, 0, j)),
            pl.BlockSpec((tm, tn), lambda j, i: (0, j)),
        ],
        out_shape=[
            jax.ShapeDtypeStruct((m, dff), BF16),
            jax.ShapeDtypeStruct((n_i, carry, dff), F32),
            jax.ShapeDtypeStruct((tm, dff), F32),
        ],
        scratch_shapes=[pltpu.VMEM((d, tn), BF16), pltpu.VMEM((d, tn), BF16),
                        pltpu.VMEM((tm + carry, tn), F32)],
        compiler_params=_params(("arbitrary", "arbitrary")),
        name=name,
    )(xn, wg[0], wu[0], conv_w, conv_b.reshape(1, dff), e1, e2)


def _gla(name, q, f, v, gate, onorm, s0, n_seq, seq_len, chunk, n_par, n_ser, hb):
    hw = q.shape[1]
    n_heads = hw // HG_DK
    dk = HG_DK
    rows = n_par * n_ser * chunk
    nt = seq_len // (n_ser * chunk)
    assert n_par == 1 or nt == 1
    n_hg = n_heads // hb
    kp = -(-3 * chunk // BF16_ROWS) * BF16_ROWS

    def body(q_ref, f_ref, v_ref, g_ref, on_ref, s0_ref, o_ref, sf_ref, st_scr):
        t = pl.program_id(2)

        @pl.when(t == 0)
        def _():
            st_scr[...] = s0_ref[...]

        row = lax.broadcasted_iota(jnp.int32, (chunk, chunk), 0)
        col = lax.broadcasted_iota(jnp.int32, (chunk, chunk), 1)
        tril = row >= col
        rr = lax.broadcasted_iota(jnp.int32, (chunk, kp), 0)
        cc = lax.broadcasted_iota(jnp.int32, (chunk, kp), 1)
        tril3 = jnp.where(jnp.logical_and(cc % chunk <= rr, cc < 3 * chunk), 1.0, 0.0).astype(BF16)
        ones = jnp.ones((BF16_ROWS, dk), BF16)

        for par in range(n_par):
            for ser in range(n_ser):
                r0 = (par * n_ser + ser) * chunk
                fc = f_ref[r0:r0 + chunk, :]
                qc = q_ref[r0:r0 + chunk, :]
                vc = v_ref[r0:r0 + chunk, :].astype(BF16)
                parts = list(_split3(jnp.log(fc)))
                if kp > 3 * chunk:
                    parts.append(jnp.zeros((kp - 3 * chunk, hb * dk), F32))
                b = _dot(tril3, jnp.concatenate(parts, axis=0).astype(BF16))
                b_end = b[chunk - 1:chunk, :]
                kc = 1.0 - fc
                q_dec = (qc * jnp.exp(b)).astype(BF16)
                k_inc = (kc * jnp.exp(-b)).astype(BF16)
                k_dec = (kc * jnp.exp(b_end - b)).astype(BF16)
                end3 = jnp.concatenate(list(_split3(b_end)) + [jnp.zeros((BF16_ROWS - 3, hb * dk), F32)],
                                       axis=0).astype(BF16)
                for hd in range(hb):
                    ls = slice(hd * dk, (hd + 1) * dk)
                    e_col = jnp.exp(_dot_tn(end3[:, ls], ones))
                    a = jnp.where(tril, _dot_nt(q_dec[:, ls], k_inc[:, ls]), 0.0)
                    st = st_scr[par, hd]
                    o = _dot(q_dec[:, ls], st.astype(BF16)) + _dot(a.astype(BF16), vc[:, ls])
                    st_scr[par, hd] = st * e_col + _dot_tn(k_dec[:, ls], vc[:, ls])
                    y = o * lax.rsqrt(jnp.mean(o * o, axis=-1, keepdims=True) + EPS) * on_ref[:, ls]
                    o_ref[r0:r0 + chunk, ls] = (y * g_ref[r0:r0 + chunk, ls]).astype(o_ref.dtype)

        @pl.when(t == nt - 1)
        def _():
            sf_ref[...] = st_scr[...]

    row_spec = pl.BlockSpec((rows, hb * dk), lambda s, h, t: (s * nt + t, h))
    st_spec = pl.BlockSpec((n_par, hb, dk, dk), lambda s, h, t: (s, h, 0, 0))
    return pl.pallas_call(
        body,
        grid=(n_seq // n_par, n_hg, nt),
        in_specs=[row_spec, row_spec, row_spec, row_spec,
                  pl.BlockSpec((1, hb * dk), lambda s, h, t: (0, h)), st_spec],
        out_specs=[row_spec, st_spec],
        out_shape=[jax.ShapeDtypeStruct((n_seq * seq_len, hw), BF16),
                   jax.ShapeDtypeStruct((n_seq, n_heads, dk, dk), F32)],
        scratch_shapes=[pltpu.VMEM((n_par, hb, dk, dk), F32)],
        compiler_params=_params(("arbitrary", "arbitrary", "arbitrary")),
        name=name,
    )(q, f, v, gate, onorm.reshape(1, hw).astype(F32), s0)


def _flash_prompt(name, q, k, v, n_b, seq, n_heads, qk_w, v_w, bq, bk):
    nq = seq // bq
    r = bq // bk

    def body(q_ref, k_ref, v_ref, o_ref):
        row = lax.broadcasted_iota(jnp.int32, (bk, bk), 0)
        col = lax.broadcasted_iota(jnp.int32, (bk, bk), 1)
        causal = col <= row

        for qi in range(nq):
            m = [jnp.full((bk, 1), -jnp.inf, F32) for _ in range(r)]
            l = [jnp.zeros((bk, 1), F32) for _ in range(r)]
            acc = [jnp.zeros((bk, v_w), F32) for _ in range(r)]

            def sweep(d0, key0, masked):
                q_rows = q_ref[qi * bq + d0 * bk:(qi + 1) * bq, :]
                s = _dot_nt(q_rows, k_ref[key0:key0 + bk, :])
                ps, alphas = [], []
                for d in range(d0, r):
                    sd = s[(d - d0) * bk:(d - d0 + 1) * bk, :]
                    if masked and d == d0:
                        sd = jnp.where(causal, sd, -jnp.inf)
                    m_new = jnp.maximum(m[d], jnp.max(sd, axis=-1, keepdims=True))
                    alpha = jnp.exp(m[d] - m_new)
                    p = jnp.exp(sd - m_new)
                    l[d] = alpha * l[d] + jnp.sum(p, axis=-1, keepdims=True)
                    m[d] = m_new
                    ps.append(p.astype(BF16))
                    alphas.append(alpha)
                pv = _dot(jnp.concatenate(ps, axis=0), v_ref[key0:key0 + bk, :])
                for d in range(d0, r):
                    acc[d] = alphas[d - d0] * acc[d] + pv[(d - d0) * bk:(d - d0 + 1) * bk, :]

            for j in range(qi * r):
                sweep(0, j * bk, False)
            for d0 in range(r):
                sweep(d0, qi * bq + d0 * bk, True)
            for d in range(r):
                o_ref[qi * bq + d * bk:qi * bq + (d + 1) * bk, :] = (acc[d] / l[d]).astype(o_ref.dtype)

    return pl.pallas_call(
        body,
        grid=(n_b, n_heads),
        in_specs=[
            pl.BlockSpec((seq, qk_w), lambda b, h: (b, h)),
            pl.BlockSpec((seq, qk_w), lambda b, h: (b, h)),
            pl.BlockSpec((seq, v_w), lambda b, h: (b, h)),
        ],
        out_specs=pl.BlockSpec((seq, v_w), lambda b, h: (b, h)),
        out_shape=jax.ShapeDtypeStruct((n_b * seq, n_heads * v_w), BF16),
        compiler_params=_params(("arbitrary", "arbitrary")),
        name=name,
    )(q, k, v)


def _decode_attention(name, ql, qp, c_new, k_new, cache_ckv, cache_kpe_t, page_table, dec_seq):
    n_b, n_r, c_w = ql.shape
    p_w = qp.shape[2]
    page = cache_ckv.shape[1]
    n_pages = page_table.shape[1]
    g = math.gcd(n_pages, PAGES_PER_STEP)
    n_ch = math.gcd(g, DECODE_CHAINS)
    gc = g // n_ch
    n_steps = n_pages // g
    n_total = n_b * n_steps
    n_new = c_new.shape[1]
    look = DECODE_SLOTS - 1

    def body(pt_ref, ql_ref, qp_ref, cn_ref, kn_ref, ckv_hbm, kpe_hbm, o_ref,
             cbuf, kbuf, sem, m_scr, l_scr, acc_scr):
        t = pl.program_id(0)
        s_i = lax.rem(t, n_steps)
        slot = lax.rem(t, DECODE_SLOTS)

        def copies(step, slot_):
            cps = []
            for idx in range(g):
                pg = pt_ref[step * g + idx]
                cps.append(pltpu.make_async_copy(ckv_hbm.at[pg], cbuf.at[slot_, idx], sem.at[slot_, 0]))
                cps.append(pltpu.make_async_copy(kpe_hbm.at[pg], kbuf.at[slot_, idx], sem.at[slot_, 1]))
            return cps

        @pl.when(t == 0)
        def _():
            for ahead in range(look):
                for cp in copies(min(ahead, n_total - 1), ahead):
                    cp.start()

        @pl.when(s_i == 0)
        def _():
            m_scr[...] = jnp.full(m_scr.shape, -jnp.inf, F32)
            l_scr[...] = jnp.zeros(l_scr.shape, F32)
            acc_scr[...] = jnp.zeros(acc_scr.shape, F32)

        for cp in copies(t, slot):
            cp.wait()

        q_l = ql_ref[0]
        q_p = qp_ref[0]

        def update(ch, s, c_bf):
            m_prev = m_scr[ch]
            m_new = jnp.maximum(m_prev, jnp.max(s, axis=-1, keepdims=True))
            alpha = jnp.exp(m_prev - m_new)
            p = jnp.exp(s - m_new)
            l_scr[ch] = alpha * l_scr[ch] + jnp.sum(p, axis=-1, keepdims=True)
            acc_scr[ch] = alpha * acc_scr[ch] + _dot(p.astype(BF16), c_bf)
            m_scr[ch] = m_new

        for ch in range(n_ch):
            c_bf = jnp.concatenate([cbuf[slot, idx].astype(BF16) for idx in range(ch * gc, (ch + 1) * gc)], axis=0)
            kt_bf = jnp.concatenate([kbuf[slot, idx].astype(BF16) for idx in range(ch * gc, (ch + 1) * gc)], axis=1)
            update(ch, _dot_nt(q_l, c_bf) + _dot(q_p, kt_bf), c_bf)

        for cp in copies(jnp.minimum(t + look, n_total - 1), lax.rem(t + look, DECODE_SLOTS)):
            cp.start()

        @pl.when(s_i == n_steps - 1)
        def _():
            cn = cn_ref[0].astype(BF16)
            kn = kn_ref[0].astype(BF16)
            s = _dot_nt(q_l, cn) + _dot_nt(q_p, kn)
            tok = lax.broadcasted_iota(jnp.int32, (n_r, n_new), 0) % dec_seq
            key = lax.broadcasted_iota(jnp.int32, (n_r, n_new), 1)
            update(0, jnp.where(key <= tok, s, -jnp.inf), cn)
            m_all = m_scr[0]
            for ch in range(1, n_ch):
                m_all = jnp.maximum(m_all, m_scr[ch])
            l_all = jnp.zeros_like(m_all)
            acc_all = jnp.zeros((n_r, c_w), F32)
            for ch in range(n_ch):
                w = jnp.exp(m_scr[ch] - m_all)
                l_all = l_all + w * l_scr[ch]
                acc_all = acc_all + w * acc_scr[ch]
            o_ref[0] = acc_all / l_all

        @pl.when(t == n_total - 1)
        def _():
            for ahead in range(1, DECODE_SLOTS):
                for cp in copies(n_total - 1, lax.rem(t + ahead, DECODE_SLOTS)):
                    cp.wait()

    grid_spec = pltpu.PrefetchScalarGridSpec(
        num_scalar_prefetch=1,
        grid=(n_total,),
        in_specs=[
            pl.BlockSpec((1, n_r, c_w), lambda t, pt: (t // n_steps, 0, 0)),
            pl.BlockSpec((1, n_r, p_w), lambda t, pt: (t // n_steps, 0, 0)),
            pl.BlockSpec((1, n_new, c_w), lambda t, pt: (t // n_steps, 0, 0)),
            pl.BlockSpec((1, n_new, p_w), lambda t, pt: (t // n_steps, 0, 0)),
            pl.BlockSpec(memory_space=pl.ANY),
            pl.BlockSpec(memory_space=pl.ANY),
        ],
        out_specs=pl.BlockSpec((1, n_r, c_w), lambda t, pt: (t // n_steps, 0, 0)),
        scratch_shapes=[pltpu.VMEM((DECODE_SLOTS, g, page, c_w), F32),
                        pltpu.VMEM((DECODE_SLOTS, g, p_w, page), F32),
                        pltpu.SemaphoreType.DMA((DECODE_SLOTS, 2)),
                        pltpu.VMEM((n_ch, n_r, 1), F32), pltpu.VMEM((n_ch, n_r, 1), F32),
                        pltpu.VMEM((n_ch, n_r, c_w), F32)],
    )
    return pl.pallas_call(
        body,
        grid_spec=grid_spec,
        out_shape=jax.ShapeDtypeStruct((n_b, n_r, c_w), F32),
        compiler_params=_params(("arbitrary",)),
        name=name,
    )(page_table.reshape(-1), ql, qp, c_new, k_new, cache_ckv, cache_kpe_t)


def _head_matmul(name, x, w, out_dtype):
    n_h, m, k = x.shape
    n = w.shape[2]

    def body(x_ref, w_ref, o_ref):
        o_ref[0] = _dot(x_ref[0].astype(BF16), w_ref[0].astype(BF16)).astype(o_ref.dtype)

    return pl.pallas_call(
        body,
        grid=(n_h,),
        in_specs=[pl.BlockSpec((1, m, k), lambda h: (h, 0, 0)), pl.BlockSpec((1, k, n), lambda h: (h, 0, 0))],
        out_specs=pl.BlockSpec((1, m, n), lambda h: (h, 0, 0)),
        out_shape=jax.ShapeDtypeStruct((n_h, m, n), out_dtype),
        compiler_params=_params(("arbitrary",)),
        name=name,
    )(x, w)


def kernel(x_prompt, x_sample, state_hgrn, state_conv, cache_ckv, cache_kpe, page_table, p_prompt, p_sample, attn_norm, ffn_norm, ple_norm, final_norm, hg_wq, hg_wf, hg_wi, hg_wg, hg_onorm, hg_wo, hg_lb_raw, kv_norm, w_dkv, ckv_norm, w_uk, w_uv, q_wd, q_norm, q_wu, mla_wo, ffn_wgate, ffn_wup, ffn_conv_w, ffn_conv_b, ffn_wdown, ple_wg, ple_we):
    n_b, seq, d = x_prompt.shape
    n_bd, dec_seq, _ = x_sample.shape
    depth = attn_norm.shape[0]
    n_a = hg_wq.shape[0]
    dff = ffn_wgate.shape[2]
    ple_dim = p_prompt.shape[3]
    kv_lora, mla_heads, nope = w_uk.shape
    v_dim = w_uv.shape[2]
    rope_dim = w_dkv.shape[1] - kv_lora
    half = rope_dim // 2
    q_lora = q_wd.shape[2]
    page = cache_ckv.shape[1]
    past_len = page_table.shape[1] * page
    hg_heads = hg_wq.shape[2] // HG_DK
    scale = float(nope + rope_dim) ** -0.5
    qk_w = 2 * LANES
    assert nope == LANES and v_dim == LANES and rope_dim <= LANES and rope_dim % 2 == 0

    mp, ms = n_b * seq, n_bd * dec_seq
    m = mp + ms
    tm = ms
    assert tm % BF16_ROWS == 0 and seq % tm == 0 and 2 <= dec_seq <= SUBLANES
    n_prompt_tiles = mp // tm
    tiles_per_seq = seq // tm
    tmt = _pick_row_tile(m)

    h = jnp.concatenate([x_prompt.reshape(mp, d), x_sample.reshape(ms, d)], axis=0)
    p_all = jnp.concatenate([p_prompt.reshape(depth, mp, ple_dim), p_sample.reshape(depth, ms, ple_dim)], axis=1)

    pos = jnp.concatenate([jnp.tile(jnp.arange(seq, dtype=jnp.int32), n_b),
                           jnp.tile(past_len + jnp.arange(dec_seq, dtype=jnp.int32), n_bd)])
    inv = ROPE_THETA ** (-jnp.arange(half, dtype=F32) / half)
    ang = pos.astype(F32)[:, None] * inv[None, :]
    cos2 = jnp.concatenate([jnp.cos(ang), jnp.cos(ang)], axis=1)
    sin2 = jnp.concatenate([-jnp.sin(ang), jnp.sin(ang)], axis=1)
    cos_pad = jnp.pad(cos2, ((0, 0), (0, LANES - rope_dim)))
    sin_pad = jnp.pad(sin2, ((0, 0), (0, LANES - rope_dim)))

    states_p, states_s, conv_p, conv_s = [], [], [], []
    c_all = kpe_all = None

    for layer in range(depth):
        xn = _rmsnorm(h, attn_norm[layer], BF16, f"attn_norm_{layer}")
        if layer < n_a:
            hw = hg_heads * HG_DK

            def qf_epilogue(accs, rows, cols, layer=layer):
                raw = cols[0]
                e = jnp.exp(raw - jnp.max(raw, axis=0, keepdims=True))
                lb = jnp.sum(e[:layer + 1], axis=0, keepdims=True) / jnp.sum(e, axis=0, keepdims=True)
                return jax.nn.silu(accs[0]), lb + (1.0 - lb) * jax.nn.sigmoid(accs[1])

            n_j = _pick_n_tiles([hw, hw], [d, d], d * 2, tmt, 2 * 4)
            q_act, f_act = _fused_matmul(
                f"hgrn_qf_{layer}", [xn], [(hg_wq, layer), (hg_wf, layer)], [0, 0], [], [hg_lb_raw.astype(F32)],
                qf_epilogue, [(F32, hw), (F32, hw)], tmt, n_j)
            v_act, g_act = _fused_matmul(
                f"hgrn_ig_{layer}", [xn], [(hg_wi, layer), (hg_wg, layer)], [0, 0], [], [],
                lambda accs, rows, cols: (accs[0], jax.nn.silu(accs[1])), [(F32, hw), (F32, hw)], tmt, n_j)

            hb = math.gcd(hg_heads, 8)
            n_ser = math.gcd(seq // HG_CHUNK, 4)
            o_p, s_p = _gla(f"gla_prompt_{layer}", q_act, f_act, v_act, g_act, hg_onorm[layer],
                            jnp.zeros((n_b, hg_heads, HG_DK, HG_DK), F32), n_b, seq, HG_CHUNK, 1, n_ser, hb)

            def pad_seq(a, fill):
                a = a[mp:].reshape(n_bd, dec_seq, hw)
                a = jnp.pad(a, ((0, 0), (0, SUBLANES - dec_seq), (0, 0)), constant_values=fill)
                return a.reshape(n_bd * SUBLANES, hw)

            o_s, s_s = _gla(f"gla_sample_{layer}", pad_seq(q_act, 0.0), pad_seq(f_act, 1.0), pad_seq(v_act, 0.0),
                            pad_seq(g_act, 0.0), hg_onorm[layer], state_hgrn[layer].astype(F32),
                            n_bd, SUBLANES, SUBLANES, math.gcd(n_bd, 4), 1, hb)
            o_s = o_s.reshape(n_bd, SUBLANES, hw)[:, :dec_seq].reshape(ms, hw)
            o_all = jnp.concatenate([o_p, o_s], axis=0)
            states_p.append(s_p)
            states_s.append(s_s.astype(state_hgrn.dtype))
            w_out = (hg_wo, layer)
        else:
            j = layer - n_a
            tmq = math.gcd(tm, 256)
            n_jq = _pick_n_tiles([q_lora], [d], d * 2, tmq, 2, align=q_lora)

            def cq_epilogue(accs, rows, cols):
                x = accs[0]
                return (x * lax.rsqrt(jnp.mean(x * x, axis=-1, keepdims=True) + EPS) * cols[0],)

            cq = _fused_matmul(f"mla_qdown_{j}", [xn], [(q_wd, j)], [0], [], [q_norm[j].reshape(1, q_lora).astype(F32)],
                               cq_epilogue, [(BF16, q_lora)], tmq, n_jq)[0]

            wq3 = q_wu[j].reshape(q_lora, mla_heads, nope + rope_dim)
            wq_pad = jnp.pad(wq3, ((0, 0), (0, 0), (0, qk_w - nope - rope_dim))).reshape(q_lora, mla_heads * qk_w)
            n_jh = _pick_n_tiles([mla_heads * qk_w], [q_lora], q_lora * 2, tmt, 2, align=qk_w)
            heads_per_tile = mla_heads // n_jh

            def q_epilogue(accs, rows, cols):
                cos_t, sin_t = rows
                lane = lax.broadcasted_iota(jnp.int32, cos_t.shape, 1)
                outs = []
                for hh in range(heads_per_tile):
                    lo = accs[0][:, hh * qk_w:hh * qk_w + LANES]
                    hi = accs[0][:, hh * qk_w + LANES:(hh + 1) * qk_w]
                    swapped = jnp.where(lane < half, pltpu.roll(hi, LANES - half, 1), pltpu.roll(hi, half, 1))
                    outs += [lo * scale, (hi * cos_t + swapped * sin_t) * scale]
                return (jnp.concatenate(outs, axis=1),)

            qfull = _fused_matmul(f"mla_qup_{j}", [cq], [(wq_pad, None)], [0], [(cos_pad, False), (sin_pad, False)], [],
                                  q_epilogue, [(BF16, mla_heads * qk_w)], tmt, n_jh)[0]

            kpe_pad = jnp.pad(kpe_all[:mp], ((0, 0), (0, LANES - rope_dim)))
            n_jk = mla_heads // 2 if mla_heads % 2 == 0 else mla_heads
            heads_k = mla_heads // n_jk

            def kv_epilogue(accs, rows, cols):
                ks = []
                for hh in range(heads_k):
                    ks += [accs[0][:, hh * nope:(hh + 1) * nope], rows[0]]
                return jnp.concatenate(ks, axis=1), accs[1]

            tmk = math.gcd(mp, 1024)
            kfull, vfull = _fused_matmul(
                f"mla_kv_up_{j}", [c_all],
                [(w_uk.reshape(kv_lora, mla_heads * nope), None), (w_uv.reshape(kv_lora, mla_heads * v_dim), None)],
                [0, 0], [(kpe_pad, False)], [], kv_epilogue,
                [(BF16, mla_heads * qk_w), (BF16, mla_heads * v_dim)], tmk, n_jk, m_rows=mp)
            bq = math.gcd(seq, 512)
            o_p = _flash_prompt(f"mla_prompt_attn_{j}", qfull, kfull, vfull, n_b, seq, mla_heads, qk_w, v_dim,
                                bq, math.gcd(bq, 256))

            q_s = qfull[mp:].reshape(n_bd, dec_seq, mla_heads, qk_w)
            q_nope_h = q_s[..., :nope].transpose(2, 0, 1, 3).reshape(mla_heads, ms, nope)
            q_lat = _head_matmul(f"mla_q_absorb_{j}", q_nope_h, w_uk.transpose(1, 2, 0), BF16)
            ql = q_lat.reshape(mla_heads, n_bd, dec_seq, kv_lora).transpose(1, 0, 2, 3).reshape(n_bd, mla_heads * dec_seq, kv_lora)
            qp = q_s[..., nope:nope + rope_dim].transpose(0, 2, 1, 3).reshape(n_bd, mla_heads * dec_seq, rope_dim)
            pad_new = ((0, 0), (0, BF16_ROWS - dec_seq), (0, 0))
            c_new = jnp.pad(c_all[mp:].reshape(n_bd, dec_seq, kv_lora), pad_new)
            k_new = jnp.pad(kpe_all[mp:].reshape(n_bd, dec_seq, rope_dim), pad_new)
            lat = _decode_attention(f"mla_sample_attn_{j}", ql, qp, c_new, k_new, cache_ckv,
                                    jnp.transpose(cache_kpe, (0, 2, 1)), page_table.astype(jnp.int32), dec_seq)
            lat_h = lat.reshape(n_bd, mla_heads, dec_seq, kv_lora).transpose(1, 0, 2, 3).reshape(mla_heads, ms, kv_lora)
            o_sh = _head_matmul(f"mla_v_up_{j}", lat_h, w_uv.transpose(1, 0, 2), BF16)
            o_s = o_sh.transpose(1, 0, 2).reshape(ms, mla_heads * v_dim)
            o_all = jnp.concatenate([o_p, o_s], axis=0)
            w_out = (mla_wo, j)

        k_out = w_out[0].shape[-2]
        n_jo = _pick_n_tiles([d], [k_out], k_out * 2, tm, 2 * 4)
        h = _fused_matmul(f"mixer_out_{layer}", [o_all], [w_out], [0], [(h, True)], [],
                          lambda accs, rows, cols: (rows[0] + accs[0],), [(F32, d)], tm, n_jo)[0]

        xn = _rmsnorm(h, ffn_norm[layer], BF16, f"ffn_norm_{layer}")
        buf = state_conv[layer].astype(F32)
        e1 = jnp.pad(buf[:, 1:2], ((0, 0), (0, dec_seq - 1), (0, 0))).reshape(ms, dff)
        e2 = jnp.pad(buf, ((0, 0), (0, dec_seq - 2), (0, 0))).reshape(ms, dff)
        n_jf = _pick_n_tiles([dff, dff], [d, d], d * 2, tm, 4 * 4)
        hid, tails, a_samp = _ffn_up(f"ffn_up_{layer}", xn, (ffn_wgate, layer), (ffn_wup, layer), ffn_conv_w[layer],
                                     ffn_conv_b[layer], e1, e2, n_prompt_tiles, tiles_per_seq, dec_seq, tm, n_jf)
        conv_p.append(tails[tiles_per_seq - 1:n_prompt_tiles:tiles_per_seq, SUBLANES - 2:, :].astype(state_conv.dtype))
        conv_s.append(a_samp.reshape(n_bd, dec_seq, dff)[:, dec_seq - 2:, :].astype(state_conv.dtype))
        tmd = math.gcd(tm, 256)
        n_jd = _pick_n_tiles([d], [dff], dff * 2, tmd, 2 * 4)
        h = _fused_matmul(f"ffn_down_{layer}", [hid], [(ffn_wdown, layer)], [0], [(h, True)], [],
                          lambda accs, rows, cols: (rows[0] + accs[0],), [(F32, d)], tmd, n_jd)[0]

        xn = _rmsnorm(h, ple_norm[layer], BF16, f"ple_norm_{layer}")
        n_jp = _pick_n_tiles([d, d], [d, ple_dim], (d + 2 * ple_dim) * 2, tmt, 2 * 4)
        h = _fused_matmul(f"ple_{layer}", [xn, p_all[layer]], [(ple_wg, layer), (ple_we, layer)], [0, 1], [(h, True)], [],
                          lambda accs, rows, cols: (rows[0] + jax.nn.sigmoid(accs[0]) * accs[1],),
                          [(F32, d)], tmt, n_jp)[0]

        if layer == n_a - 1:
            xk = _rmsnorm(h, kv_norm, BF16, "kv_norm")
            swap = jnp.concatenate([jnp.arange(half, rope_dim), jnp.arange(half)])
            w_pe = w_dkv[:, kv_lora:]

            def kvs_epilogue(accs, rows, cols):
                x = accs[0]
                c = x * lax.rsqrt(jnp.mean(x * x, axis=-1, keepdims=True) + EPS) * cols[0]
                return c, accs[1] * rows[0] + accs[2] * rows[1]

            c_all, kpe_all = _fused_matmul(
                "kv_stream", [xk], [(w_dkv[:, :kv_lora], None), (w_pe, None), (w_pe[:, swap], None)], [0, 0, 0],
                [(cos2, False), (sin2, False)], [ckv_norm.reshape(1, kv_lora).astype(F32)],
                kvs_epilogue, [(F32, kv_lora), (F32, rope_dim)], tm, 1)

    y = _rmsnorm(h, final_norm, x_prompt.dtype, "final_norm")
    return (y[:mp].reshape(n_b, seq, d), y[mp:].reshape(n_bd, dec_seq, d),
            jnp.stack(states_p).astype(state_hgrn.dtype), jnp.stack(states_s),
            jnp.stack(conv_p), jnp.stack(conv_s),
            c_all[:mp].reshape(n_b, seq, kv_lora), c_all[mp:].reshape(n_bd, dec_seq, kv_lora),
            kpe_all[:mp].reshape(n_b, seq, rope_dim), kpe_all[mp:].reshape(n_bd, dec_seq, rope_dim))
```

```python
import math

import jax
import jax.numpy as jnp
from jax import lax
from jax.experimental import pallas as pl
from jax.experimental.pallas import tpu as pltpu

F32 = jnp.float32
BF16 = jnp.bfloat16
EPS = 1e-6
ROPE_THETA = 10000.0
HG_DK = 128
HG_CHUNK = 32
SUBLANES = 8
LANES = 128
BF16_ROWS = 16
VMEM_LIMIT = 56 * 1024 * 1024
VMEM_BUDGET = 46 * 1024 * 1024
PAGES_PER_STEP = 32
DECODE_CHAINS = 1
DECODE_SLOTS = 3
MAX_ROW_TILE = 1152


def _params(sem):
    return pltpu.CompilerParams(dimension_semantics=sem, vmem_limit_bytes=VMEM_LIMIT)


def _cast_rows(w_ref, wbf_ref):
    k = w_ref.shape[0]
    ch = math.gcd(k, 512)

    def step(c, carry):
        r = pl.multiple_of(c * ch, ch)
        wbf_ref[pl.ds(r, ch), :] = w_ref[pl.ds(r, ch), :].astype(BF16)
        return carry

    lax.fori_loop(0, k // ch, step, 0)


def _dot(a, b):
    return jnp.dot(a, b, preferred_element_type=F32)


def _dot_nt(a, b):
    return lax.dot_general(a, b, (((1,), (1,)), ((), ())), preferred_element_type=F32)


def _dot_tn(a, b):
    return lax.dot_general(a, b, (((0,), (0,)), ((), ())), preferred_element_type=F32)


def _split3(x):
    p1 = x.astype(BF16).astype(F32)
    r1 = x - p1
    p2 = r1.astype(BF16).astype(F32)
    p3 = (r1 - p2).astype(BF16).astype(F32)
    return p1, p2, p3


def _weight_spec(w, n_j):
    arr, layer = w
    k, n = arr.shape[-2:]
    if arr.ndim == 3:
        return pl.BlockSpec((None, k, n // n_j), lambda j, i: (layer, 0, j))
    return pl.BlockSpec((k, n // n_j), lambda j, i: (0, j))


def _rmsnorm_body(h_ref, g_ref, o_ref):
    x = h_ref[...]
    inv = lax.rsqrt(jnp.mean(x * x, axis=-1, keepdims=True) + EPS)
    o_ref[...] = (x * inv * g_ref[...]).astype(o_ref.dtype)


def _rmsnorm(h, g, out_dtype, name, row0=0, n_rows=None):
    d = h.shape[1]
    m = h.shape[0] if n_rows is None else n_rows
    tr = math.gcd(math.gcd(m, row0), 256)
    blk0 = row0 // tr
    return pl.pallas_call(
        _rmsnorm_body,
        grid=(m // tr,),
        in_specs=[pl.BlockSpec((tr, d), lambda i: (i + blk0, 0)), pl.BlockSpec((1, d), lambda i: (0, 0))],
        out_specs=pl.BlockSpec((tr, d), lambda i: (i, 0)),
        out_shape=jax.ShapeDtypeStruct((m, d), out_dtype),
        compiler_params=_params(("arbitrary",)),
        name=name,
    )(h, g.reshape(1, d).astype(F32))


def _fused_matmul(name, xs, ws, w_x, row_ins, col_ins, epilogue, outs, tm, n_j, m_rows=None):
    m = xs[0].shape[0] if m_rows is None else m_rows
    n_i = m // tm
    nx, nw, nr, nc, no = len(xs), len(ws), len(row_ins), len(col_ins), len(outs)
    needs_copy = [w.dtype != BF16 for w, _ in ws]

    def body(*refs):
        x_refs = refs[:nx]
        w_refs = refs[nx:nx + nw]
        r_refs = refs[nx + nw:nx + nw + nr]
        c_refs = refs[nx + nw + nr:nx + nw + nr + nc]
        o_refs = refs[nx + nw + nr + nc:nx + nw + nr + nc + no]
        scratch_refs = list(refs[nx + nw + nr + nc + no:])
        wbf_refs = [scratch_refs.pop(0) if needs_copy[j] else w_refs[j] for j in range(nw)]

        @pl.when(pl.program_id(1) == 0)
        def _():
            for j in range(nw):
                if needs_copy[j]:
                    _cast_rows(w_refs[j], wbf_refs[j])

        xv = [x[...].astype(BF16) for x in x_refs]
        accs = [_dot(xv[w_x[j]], wbf_refs[j][...]) for j in range(nw)]
        res = epilogue(accs, [r[...] for r in r_refs], [c[...] for c in c_refs])
        for o_ref, r in zip(o_refs, res):
            o_ref[...] = r.astype(o_ref.dtype)

    in_specs = [pl.BlockSpec((tm, x.shape[1]), lambda j, i: (i, 0)) for x in xs]
    in_specs += [_weight_spec(w, n_j) for w in ws]
    for arr, tiled in row_ins:
        if tiled:
            in_specs.append(pl.BlockSpec((tm, arr.shape[1] // n_j), lambda j, i: (i, j)))
        else:
            in_specs.append(pl.BlockSpec((tm, arr.shape[1]), lambda j, i: (i, 0)))
    in_specs += [pl.BlockSpec((c.shape[0], c.shape[1] // n_j), lambda j, i: (0, j)) for c in col_ins]
    out_specs = [pl.BlockSpec((tm, w // n_j), lambda j, i: (i, j)) for _, w in outs]
    out_shape = [jax.ShapeDtypeStruct((m, w), dt) for dt, w in outs]
    scratch = [pltpu.VMEM((w.shape[-2], w.shape[-1] // n_j), BF16) for (w, _), c in zip(ws, needs_copy) if c]
    return pl.pallas_call(
        body,
        grid=(n_j, n_i),
        in_specs=in_specs,
        out_specs=out_specs,
        out_shape=out_shape,
        scratch_shapes=scratch,
        compiler_params=_params(("arbitrary", "arbitrary")),
        name=name,
    )(*xs, *[w for w, _ in ws], *[a for a, _ in row_ins], *col_ins)


def _pick_n_tiles(widths, ks, x_row_bytes, tm, io_col_bytes, align=LANES, bf16_weights=False):
    w_elem_bytes = 2 * 2 if bf16_weights else 2 * 4 + 2
    for n in range(1, max(widths) + 1):
        if any(w % n or (w // n) % align for w in widths):
            continue
        tns = [w // n for w in widths]
        wbytes = sum(k * t * w_elem_bytes for k, t in zip(ks, tns))
        xbytes = 2 * tm * x_row_bytes
        iobytes = 2 * tm * max(tns) * io_col_bytes
        if wbytes + xbytes + iobytes <= VMEM_BUDGET and max(tns) <= 1024:
            return n
    raise ValueError("no column tiling fits the VMEM budget")


def _pick_row_tile(m):
    for tm in range(min(m, MAX_ROW_TILE), 0, -1):
        if m % tm == 0 and tm % BF16_ROWS == 0:
            return tm
    raise ValueError("row count has no bf16-tile-aligned divisor")


def _ffn_up(name, xn, wg, wu, conv_w, conv_b, e1, e2, n_prompt_tiles, tiles_per_seq, dec_seq, tm, n_j):
    m, d = xn.shape
    dff = wg[0].shape[-1]
    tn = dff // n_j
    n_i = m // tm
    carry = SUBLANES

    def body(x_ref, wg_ref, wu_ref, cw_ref, cb_ref, e1_ref, e2_ref, hid_ref, tail_ref, asamp_ref,
             wg_bf, wu_bf, ext):
        i = pl.program_id(1)

        @pl.when(i == 0)
        def _():
            _cast_rows(wg_ref, wg_bf)
            _cast_rows(wu_ref, wu_bf)
            ext[0:carry, :] = jnp.zeros((carry, tn), F32)

        x = x_ref[...]
        a = _dot(x, wg_bf[...])
        u = _dot(x, wu_bf[...])
        is_sample = i >= n_prompt_tiles
        fresh = jnp.logical_or(i % tiles_per_seq == 0, is_sample)
        ext[0:carry, :] = jnp.where(fresh, 0.0, ext[0:carry, :])
        ext[carry:carry + tm, :] = a
        t = lax.broadcasted_iota(jnp.int32, (tm, tn), 0) % dec_seq
        prev1 = jnp.where(jnp.logical_and(is_sample, t == 0), e1_ref[...], ext[carry - 1:carry - 1 + tm, :])
        prev2 = jnp.where(jnp.logical_and(is_sample, t <= 1), e2_ref[...], ext[carry - 2:carry - 2 + tm, :])
        cw = cw_ref[...]
        conv = cb_ref[...] + prev2 * cw[0:1, :] + prev1 * cw[1:2, :] + a * cw[2:3, :]
        gelu = 0.5 * conv * (1.0 + lax.erf(conv * math.sqrt(0.5)))
        hid_ref[...] = (gelu * u).astype(hid_ref.dtype)
        asamp_ref[...] = a
        tail = a[tm - carry:tm, :]
        ext[0:carry, :] = tail
        tail_ref[0] = tail

    return pl.pallas_call(
        body,
        grid=(n_j, n_i),
        in_specs=[
            pl.BlockSpec((tm, d), lambda j, i: (i, 0)),
            _weight_spec(wg, n_j),
            _weight_spec(wu, n_j),
            pl.BlockSpec((conv_w.shape[0], tn), lambda j, i: (0, j)),
            pl.BlockSpec((1, tn), lambda j, i: (0, j)),
            pl.BlockSpec((tm, tn), lambda j, i: (0, j)),
            pl.BlockSpec((tm, tn), lambda j, i: (0, j)),
        ],
        out_specs=[
            pl.BlockSpec((tm, tn), lambda j, i: (i, j)),
            pl.BlockSpec((1, carry, tn), lambda j, i: (i, 0, j)),
            pl.BlockSpec((tm, tn), lambda j, i: (0, j)),
        ],
        out_shape=[
            jax.ShapeDtypeStruct((m, dff), BF16),
            jax.ShapeDtypeStruct((n_i, carry, dff), F32),
            jax.ShapeDtypeStruct((tm, dff), F32),
        ],
        scratch_shapes=[pltpu.VMEM((d, tn), BF16), pltpu.VMEM((d, tn), BF16),
                        pltpu.VMEM((tm + carry, tn), F32)],
        compiler_params=_params(("arbitrary", "arbitrary")),
        name=name,
    )(xn, wg[0], wu[0], conv_w, conv_b.reshape(1, dff), e1, e2)


def _gla(name, q, f, v, gate, onorm, s0, n_seq, seq_len, chunk, n_par, n_ser, hb):
    hw = q.shape[1]
    n_heads = hw // HG_DK
    dk = HG_DK
    rows = n_par * n_ser * chunk
    nt = seq_len // (n_ser * chunk)
    assert n_par == 1 or nt == 1
    n_hg = n_heads // hb
    kp = -(-3 * chunk // BF16_ROWS) * BF16_ROWS

    def body(q_ref, f_ref, v_ref, g_ref, on_ref, s0_ref, o_ref, sf_ref, st_scr):
        t = pl.program_id(2)

        @pl.when(t == 0)
        def _():
            st_scr[...] = s0_ref[...]

        row = lax.broadcasted_iota(jnp.int32, (chunk, chunk), 0)
        col = lax.broadcasted_iota(jnp.int32, (chunk, chunk), 1)
        tril = row >= col
        rr = lax.broadcasted_iota(jnp.int32, (chunk, kp), 0)
        cc = lax.broadcasted_iota(jnp.int32, (chunk, kp), 1)
        tril3 = jnp.where(jnp.logical_and(cc % chunk <= rr, cc < 3 * chunk), 1.0, 0.0).astype(BF16)
        ones = jnp.ones((BF16_ROWS, dk), BF16)

        for par in range(n_par):
            for ser in range(n_ser):
                r0 = (par * n_ser + ser) * chunk
                fc = f_ref[r0:r0 + chunk, :]
                qc = q_ref[r0:r0 + chunk, :]
                vc = v_ref[r0:r0 + chunk, :].astype(BF16)
                parts = list(_split3(jnp.log(fc)))
                if kp > 3 * chunk:
                    parts.append(jnp.zeros((kp - 3 * chunk, hb * dk), F32))
                b = _dot(tril3, jnp.concatenate(parts, axis=0).astype(BF16))
                b_end = b[chunk - 1:chunk, :]
                kc = 1.0 - fc
                q_dec = (qc * jnp.exp(b)).astype(BF16)
                k_inc = (kc * jnp.exp(-b)).astype(BF16)
                k_dec = (kc * jnp.exp(b_end - b)).astype(BF16)
                end3 = jnp.concatenate(list(_split3(b_end)) + [jnp.zeros((BF16_ROWS - 3, hb * dk), F32)],
                                       axis=0).astype(BF16)
                for hd in range(hb):
                    ls = slice(hd * dk, (hd + 1) * dk)
                    e_col = jnp.exp(_dot_tn(end3[:, ls], ones))
                    a = jnp.where(tril, _dot_nt(q_dec[:, ls], k_inc[:, ls]), 0.0)
                    st = st_scr[par, hd]
                    o = _dot(q_dec[:, ls], st.astype(BF16)) + _dot(a.astype(BF16), vc[:, ls])
                    st_scr[par, hd] = st * e_col + _dot_tn(k_dec[:, ls], vc[:, ls])
                    y = o * lax.rsqrt(jnp.mean(o * o, axis=-1, keepdims=True) + EPS) * on_ref[:, ls]
                    o_ref[r0:r0 + chunk, ls] = (y * g_ref[r0:r0 + chunk, ls]).astype(o_ref.dtype)

        @pl.when(t == nt - 1)
        def _():
            sf_ref[...] = st_scr[...]

    row_spec = pl.BlockSpec((rows, hb * dk), lambda s, h, t: (s * nt + t, h))
    st_spec = pl.BlockSpec((n_par, hb, dk, dk), lambda s, h, t: (s, h, 0, 0))
    return pl.pallas_call(
        body,
        grid=(n_seq // n_par, n_hg, nt),
        in_specs=[row_spec, row_spec, row_spec, row_spec,
                  pl.BlockSpec((1, hb * dk), lambda s, h, t: (0, h)), st_spec],
        out_specs=[row_spec, st_spec],
        out_shape=[jax.ShapeDtypeStruct((n_seq * seq_len, hw), BF16),
                   jax.ShapeDtypeStruct((n_seq, n_heads, dk, dk), F32)],
        scratch_shapes=[pltpu.VMEM((n_par, hb, dk, dk), F32)],
        compiler_params=_params(("arbitrary", "arbitrary", "arbitrary")),
        name=name,
    )(q, f, v, gate, onorm.reshape(1, hw).astype(F32), s0)


def _flash_prompt(name, q, k, v, n_b, seq, n_heads, qk_w, v_w, bq, bk):
    nq = seq // bq
    r = bq // bk

    def body(q_ref, k_ref, v_ref, o_ref):
        row = lax.broadcasted_iota(jnp.int32, (bk, bk), 0)
        col = lax.broadcasted_iota(jnp.int32, (bk, bk), 1)
        causal = col <= row

        for qi in range(nq):
            m = [jnp.full((bk, 1), -jnp.inf, F32) for _ in range(r)]
            l = [jnp.zeros((bk, 1), F32) for _ in range(r)]
            acc = [jnp.zeros((bk, v_w), F32) for _ in range(r)]

            def sweep(d0, key0, masked):
                q_rows = q_ref[qi * bq + d0 * bk:(qi + 1) * bq, :]
                s = _dot_nt(q_rows, k_ref[key0:key0 + bk, :])
                ps, alphas = [], []
                for d in range(d0, r):
                    sd = s[(d - d0) * bk:(d - d0 + 1) * bk, :]
                    if masked and d == d0:
                        sd = jnp.where(causal, sd, -jnp.inf)
                    m_new = jnp.maximum(m[d], jnp.max(sd, axis=-1, keepdims=True))
                    alpha = jnp.exp(m[d] - m_new)
                    p = jnp.exp(sd - m_new)
                    l[d] = alpha * l[d] + jnp.sum(p, axis=-1, keepdims=True)
                    m[d] = m_new
                    ps.append(p.astype(BF16))
                    alphas.append(alpha)
                pv = _dot(jnp.concatenate(ps, axis=0), v_ref[key0:key0 + bk, :])
                for d in range(d0, r):
                    acc[d] = alphas[d - d0] * acc[d] + pv[(d - d0) * bk:(d - d0 + 1) * bk, :]

            for j in range(qi * r):
                sweep(0, j * bk, False)
            for d0 in range(r):
                sweep(d0, qi * bq + d0 * bk, True)
            for d in range(r):
                o_ref[qi * bq + d * bk:qi * bq + (d + 1) * bk, :] = (acc[d] / l[d]).astype(o_ref.dtype)

    return pl.pallas_call(
        body,
        grid=(n_b, n_heads),
        in_specs=[
            pl.BlockSpec((seq, qk_w), lambda b, h: (b, h)),
            pl.BlockSpec((seq, qk_w), lambda b, h: (b, h)),
            pl.BlockSpec((seq, v_w), lambda b, h: (b, h)),
        ],
        out_specs=pl.BlockSpec((seq, v_w), lambda b, h: (b, h)),
        out_shape=jax.ShapeDtypeStruct((n_b * seq, n_heads * v_w), BF16),
        compiler_params=_params(("arbitrary", "arbitrary")),
        name=name,
    )(q, k, v)


def _decode_attention(name, ql, qp, c_new, k_new, cache_ckv, cache_kpe_t, page_table, dec_seq):
    n_b, n_r, c_w = ql.shape
    p_w = qp.shape[2]
    page = cache_ckv.shape[1]
    n_pages = page_table.shape[1]
    g = math.gcd(n_pages, PAGES_PER_STEP)
    n_ch = math.gcd(g, DECODE_CHAINS)
    gc = g // n_ch
    n_steps = n_pages // g
    n_total = n_b * n_steps
    n_new = c_new.shape[1]
    look = DECODE_SLOTS - 1

    def body(pt_ref, ql_ref, qp_ref, cn_ref, kn_ref, ckv_hbm, kpe_hbm, o_ref,
             cbuf, kbuf, sem, m_scr, l_scr, acc_scr):
        t = pl.program_id(0)
        s_i = lax.rem(t, n_steps)
        slot = lax.rem(t, DECODE_SLOTS)

        def copies(step, slot_):
            cps = []
            for idx in range(g):
                pg = pt_ref[step * g + idx]
                cps.append(pltpu.make_async_copy(ckv_hbm.at[pg], cbuf.at[slot_, idx], sem.at[slot_, 0]))
                cps.append(pltpu.make_async_copy(kpe_hbm.at[pg], kbuf.at[slot_, idx], sem.at[slot_, 1]))
            return cps

        @pl.when(t == 0)
        def _():
            for ahead in range(look):
                for cp in copies(min(ahead, n_total - 1), ahead):
                    cp.start()

        @pl.when(s_i == 0)
        def _():
            m_scr[...] = jnp.full(m_scr.shape, -jnp.inf, F32)
            l_scr[...] = jnp.zeros(l_scr.shape, F32)
            acc_scr[...] = jnp.zeros(acc_scr.shape, F32)

        for cp in copies(t, slot):
            cp.wait()

        q_l = ql_ref[0]
        q_p = qp_ref[0]

        def update(ch, s, c_bf):
            m_prev = m_scr[ch]
            m_new = jnp.maximum(m_prev, jnp.max(s, axis=-1, keepdims=True))
            alpha = jnp.exp(m_prev - m_new)
            p = jnp.exp(s - m_new)
            l_scr[ch] = alpha * l_scr[ch] + jnp.sum(p, axis=-1, keepdims=True)
            acc_scr[ch] = alpha * acc_scr[ch] + _dot(p.astype(BF16), c_bf)
            m_scr[ch] = m_new

        for ch in range(n_ch):
            c_bf = jnp.concatenate([cbuf[slot, idx].astype(BF16) for idx in range(ch * gc, (ch + 1) * gc)], axis=0)
            kt_bf = jnp.concatenate([kbuf[slot, idx].astype(BF16) for idx in range(ch * gc, (ch + 1) * gc)], axis=1)
            update(ch, _dot_nt(q_l, c_bf) + _dot(q_p, kt_bf), c_bf)

        for cp in copies(jnp.minimum(t + look, n_total - 1), lax.rem(t + look, DECODE_SLOTS)):
            cp.start()

        @pl.when(s_i == n_steps - 1)
        def _():
            cn = cn_ref[0].astype(BF16)
            kn = kn_ref[0].astype(BF16)
            s = _dot_nt(q_l, cn) + _dot_nt(q_p, kn)
            tok = lax.broadcasted_iota(jnp.int32, (n_r, n_new), 0) % dec_seq
            key = lax.broadcasted_iota(jnp.int32, (n_r, n_new), 1)
            update(0, jnp.where(key <= tok, s, -jnp.inf), cn)
            m_all = m_scr[0]
            for ch in range(1, n_ch):
                m_all = jnp.maximum(m_all, m_scr[ch])
            l_all = jnp.zeros_like(m_all)
            acc_all = jnp.zeros((n_r, c_w), F32)
            for ch in range(n_ch):
                w = jnp.exp(m_scr[ch] - m_all)
                l_all = l_all + w * l_scr[ch]
                acc_all = acc_all + w * acc_scr[ch]
            o_ref[0] = acc_all / l_all

        @pl.when(t == n_total - 1)
        def _():
            for ahead in range(1, DECODE_SLOTS):
                for cp in copies(n_total - 1, lax.rem(t + ahead, DECODE_SLOTS)):
                    cp.wait()

    grid_spec = pltpu.PrefetchScalarGridSpec(
        num_scalar_prefetch=1,
        grid=(n_total,),
        in_specs=[
            pl.BlockSpec((1, n_r, c_w), lambda t, pt: (t // n_steps, 0, 0)),
            pl.BlockSpec((1, n_r, p_w), lambda t, pt: (t // n_steps, 0, 0)),
            pl.BlockSpec((1, n_new, c_w), lambda t, pt: (t // n_steps, 0, 0)),
            pl.BlockSpec((1, n_new, p_w), lambda t, pt: (t // n_steps, 0, 0)),
            pl.BlockSpec(memory_space=pl.ANY),
            pl.BlockSpec(memory_space=pl.ANY),
        ],
        out_specs=pl.BlockSpec((1, n_r, c_w), lambda t, pt: (t // n_steps, 0, 0)),
        scratch_shapes=[pltpu.VMEM((DECODE_SLOTS, g, page, c_w), F32),
                        pltpu.VMEM((DECODE_SLOTS, g, p_w, page), F32),
                        pltpu.SemaphoreType.DMA((DECODE_SLOTS, 2)),
                        pltpu.VMEM((n_ch, n_r, 1), F32), pltpu.VMEM((n_ch, n_r, 1), F32),
                        pltpu.VMEM((n_ch, n_r, c_w), F32)],
    )
    return pl.pallas_call(
        body,
        grid_spec=grid_spec,
        out_shape=jax.ShapeDtypeStruct((n_b, n_r, c_w), F32),
        compiler_params=_params(("arbitrary",)),
        name=name,
    )(page_table.reshape(-1), ql, qp, c_new, k_new, cache_ckv, cache_kpe_t)


def _head_matmul(name, x, w, out_dtype):
    n_h, m, k = x.shape
    n = w.shape[2]

    def body(x_ref, w_ref, o_ref):
        o_ref[0] = _dot(x_ref[0].astype(BF16), w_ref[0].astype(BF16)).astype(o_ref.dtype)

    return pl.pallas_call(
        body,
        grid=(n_h,),
        in_specs=[pl.BlockSpec((1, m, k), lambda h: (h, 0, 0)), pl.BlockSpec((1, k, n), lambda h: (h, 0, 0))],
        out_specs=pl.BlockSpec((1, m, n), lambda h: (h, 0, 0)),
        out_shape=jax.ShapeDtypeStruct((n_h, m, n), out_dtype),
        compiler_params=_params(("arbitrary",)),
        name=name,
    )(x, w)


def kernel(x_prompt, x_sample, state_hgrn, state_conv, cache_ckv, cache_kpe, page_table, p_prompt, p_sample, attn_norm, ffn_norm, ple_norm, final_norm, hg_wq, hg_wf, hg_wi, hg_wg, hg_onorm, hg_wo, hg_lb_raw, kv_norm, w_dkv, ckv_norm, w_uk, w_uv, q_wd, q_norm, q_wu, mla_wo, ffn_wgate, ffn_wup, ffn_conv_w, ffn_conv_b, ffn_wdown, ple_wg, ple_we):
    n_b, seq, d = x_prompt.shape
    n_bd, dec_seq, _ = x_sample.shape
    depth = attn_norm.shape[0]
    n_a = hg_wq.shape[0]
    dff = ffn_wgate.shape[2]
    ple_dim = p_prompt.shape[3]
    kv_lora, mla_heads, nope = w_uk.shape
    v_dim = w_uv.shape[2]
    rope_dim = w_dkv.shape[1] - kv_lora
    half = rope_dim // 2
    q_lora = q_wd.shape[2]
    page = cache_ckv.shape[1]
    past_len = page_table.shape[1] * page
    hg_heads = hg_wq.shape[2] // HG_DK
    scale = float(nope + rope_dim) ** -0.5
    qk_w = 2 * LANES
    assert nope == LANES and v_dim == LANES and rope_dim <= LANES and rope_dim % 2 == 0

    mp, ms = n_b * seq, n_bd * dec_seq
    m = mp + ms
    tm = ms
    assert tm % BF16_ROWS == 0 and seq % tm == 0 and 2 <= dec_seq <= SUBLANES
    n_prompt_tiles = mp // tm
    tiles_per_seq = seq // tm
    tmt = _pick_row_tile(m)

    h = jnp.concatenate([x_prompt.reshape(mp, d), x_sample.reshape(ms, d)], axis=0)
    p_all = jnp.concatenate([p_prompt.reshape(depth, mp, ple_dim), p_sample.reshape(depth, ms, ple_dim)], axis=1)

    pos = jnp.concatenate([jnp.tile(jnp.arange(seq, dtype=jnp.int32), n_b),
                           jnp.tile(past_len + jnp.arange(dec_seq, dtype=jnp.int32), n_bd)])
    inv = ROPE_THETA ** (-jnp.arange(half, dtype=F32) / half)
    ang = pos.astype(F32)[:, None] * inv[None, :]
    cos2 = jnp.concatenate([jnp.cos(ang), jnp.cos(ang)], axis=1)
    sin2 = jnp.concatenate([-jnp.sin(ang), jnp.sin(ang)], axis=1)
    cos_pad = jnp.pad(cos2, ((0, 0), (0, LANES - rope_dim)))
    sin_pad = jnp.pad(sin2, ((0, 0), (0, LANES - rope_dim)))

    wdown_bf = ffn_wdown.astype(BF16)

    states_p, states_s, conv_p, conv_s = [], [], [], []
    c_all = kpe_all = None

    for layer in range(depth):
        xn = _rmsnorm(h, attn_norm[layer], BF16, f"attn_norm_{layer}")
        if layer < n_a:
            hw = hg_heads * HG_DK

            def qf_epilogue(accs, rows, cols, layer=layer):
                raw = cols[0]
                e = jnp.exp(raw - jnp.max(raw, axis=0, keepdims=True))
                lb = jnp.sum(e[:layer + 1], axis=0, keepdims=True) / jnp.sum(e, axis=0, keepdims=True)
                return jax.nn.silu(accs[0]), lb + (1.0 - lb) * jax.nn.sigmoid(accs[1])

            n_j = _pick_n_tiles([hw, hw], [d, d], d * 2, tmt, 2 * 4)
            q_act, f_act = _fused_matmul(
                f"hgrn_qf_{layer}", [xn], [(hg_wq, layer), (hg_wf, layer)], [0, 0], [], [hg_lb_raw.astype(F32)],
                qf_epilogue, [(F32, hw), (F32, hw)], tmt, n_j)
            v_act, g_act = _fused_matmul(
                f"hgrn_ig_{layer}", [xn], [(hg_wi, layer), (hg_wg, layer)], [0, 0], [], [],
                lambda accs, rows, cols: (accs[0], jax.nn.silu(accs[1])), [(F32, hw), (F32, hw)], tmt, n_j)

            hb = math.gcd(hg_heads, 8)
            n_ser = math.gcd(seq // HG_CHUNK, 4)
            o_p, s_p = _gla(f"gla_prompt_{layer}", q_act, f_act, v_act, g_act, hg_onorm[layer],
                            jnp.zeros((n_b, hg_heads, HG_DK, HG_DK), F32), n_b, seq, HG_CHUNK, 1, n_ser, hb)

            def pad_seq(a, fill):
                a = a[mp:].reshape(n_bd, dec_seq, hw)
                a = jnp.pad(a, ((0, 0), (0, SUBLANES - dec_seq), (0, 0)), constant_values=fill)
                return a.reshape(n_bd * SUBLANES, hw)

            o_s, s_s = _gla(f"gla_sample_{layer}", pad_seq(q_act, 0.0), pad_seq(f_act, 1.0), pad_seq(v_act, 0.0),
                            pad_seq(g_act, 0.0), hg_onorm[layer], state_hgrn[layer].astype(F32),
                            n_bd, SUBLANES, SUBLANES, math.gcd(n_bd, 4), 1, hb)
            o_s = o_s.reshape(n_bd, SUBLANES, hw)[:, :dec_seq].reshape(ms, hw)
            o_all = jnp.concatenate([o_p, o_s], axis=0)
            states_p.append(s_p)
            states_s.append(s_s.astype(state_hgrn.dtype))
            w_out = (hg_wo, layer)
        else:
            j = layer - n_a
            tmq = math.gcd(tm, 256)
            n_jq = _pick_n_tiles([q_lora], [d], d * 2, tmq, 2, align=q_lora)

            def cq_epilogue(accs, rows, cols):
                x = accs[0]
                return (x * lax.rsqrt(jnp.mean(x * x, axis=-1, keepdims=True) + EPS) * cols[0],)

            cq = _fused_matmul(f"mla_qdown_{j}", [xn], [(q_wd, j)], [0], [], [q_norm[j].reshape(1, q_lora).astype(F32)],
                               cq_epilogue, [(BF16, q_lora)], tmq, n_jq)[0]

            wq3 = q_wu[j].reshape(q_lora, mla_heads, nope + rope_dim)
            wq_pad = jnp.pad(wq3, ((0, 0), (0, 0), (0, qk_w - nope - rope_dim))).reshape(q_lora, mla_heads * qk_w)
            n_jh = _pick_n_tiles([mla_heads * qk_w], [q_lora], q_lora * 2, tmt, 2, align=qk_w)
            heads_per_tile = mla_heads // n_jh

            def q_epilogue(accs, rows, cols):
                cos_t, sin_t = rows
                lane = lax.broadcasted_iota(jnp.int32, cos_t.shape, 1)
                outs = []
                for hh in range(heads_per_tile):
                    lo = accs[0][:, hh * qk_w:hh * qk_w + LANES]
                    hi = accs[0][:, hh * qk_w + LANES:(hh + 1) * qk_w]
                    swapped = jnp.where(lane < half, pltpu.roll(hi, LANES - half, 1), pltpu.roll(hi, half, 1))
                    outs += [lo * scale, (hi * cos_t + swapped * sin_t) * scale]
                return (jnp.concatenate(outs, axis=1),)

            qfull = _fused_matmul(f"mla_qup_{j}", [cq], [(wq_pad, None)], [0], [(cos_pad, False), (sin_pad, False)], [],
                                  q_epilogue, [(BF16, mla_heads * qk_w)], tmt, n_jh)[0]

            kpe_pad = jnp.pad(kpe_all[:mp], ((0, 0), (0, LANES - rope_dim)))
            n_jk = mla_heads // 2 if mla_heads % 2 == 0 else mla_heads
            heads_k = mla_heads // n_jk

            def kv_epilogue(accs, rows, cols):
                ks = []
                for hh in range(heads_k):
                    ks += [accs[0][:, hh * nope:(hh + 1) * nope], rows[0]]
                return jnp.concatenate(ks, axis=1), accs[1]

            tmk = math.gcd(mp, 1024)
            kfull, vfull = _fused_matmul(
                f"mla_kv_up_{j}", [c_all],
                [(w_uk.reshape(kv_lora, mla_heads * nope), None), (w_uv.reshape(kv_lora, mla_heads * v_dim), None)],
                [0, 0], [(kpe_pad, False)], [], kv_epilogue,
                [(BF16, mla_heads * qk_w), (BF16, mla_heads * v_dim)], tmk, n_jk, m_rows=mp)
            bq = math.gcd(seq, 512)
            o_p = _flash_prompt(f"mla_prompt_attn_{j}", qfull, kfull, vfull, n_b, seq, mla_heads, qk_w, v_dim,
                                bq, math.gcd(bq, 256))

            q_s = qfull[mp:].reshape(n_bd, dec_seq, mla_heads, qk_w)
            q_nope_h = q_s[..., :nope].transpose(2, 0, 1, 3).reshape(mla_heads, ms, nope)
            q_lat = _head_matmul(f"mla_q_absorb_{j}", q_nope_h, w_uk.transpose(1, 2, 0), BF16)
            ql = q_lat.reshape(mla_heads, n_bd, dec_seq, kv_lora).transpose(1, 0, 2, 3).reshape(n_bd, mla_heads * dec_seq, kv_lora)
            qp = q_s[..., nope:nope + rope_dim].transpose(0, 2, 1, 3).reshape(n_bd, mla_heads * dec_seq, rope_dim)
            pad_new = ((0, 0), (0, BF16_ROWS - dec_seq), (0, 0))
            c_new = jnp.pad(c_all[mp:].reshape(n_bd, dec_seq, kv_lora), pad_new)
            k_new = jnp.pad(kpe_all[mp:].reshape(n_bd, dec_seq, rope_dim), pad_new)
            lat = _decode_attention(f"mla_sample_attn_{j}", ql, qp, c_new, k_new, cache_ckv,
                                    jnp.transpose(cache_kpe, (0, 2, 1)), page_table.astype(jnp.int32), dec_seq)
            lat_h = lat.reshape(n_bd, mla_heads, dec_seq, kv_lora).transpose(1, 0, 2, 3).reshape(mla_heads, ms, kv_lora)
            o_sh = _head_matmul(f"mla_v_up_{j}", lat_h, w_uv.transpose(1, 0, 2), BF16)
            o_s = o_sh.transpose(1, 0, 2).reshape(ms, mla_heads * v_dim)
            o_all = jnp.concatenate([o_p, o_s], axis=0)
            w_out = (mla_wo, j)

        k_out = w_out[0].shape[-2]
        n_jo = _pick_n_tiles([d], [k_out], k_out * 2, tm, 2 * 4)
        h = _fused_matmul(f"mixer_out_{layer}", [o_all], [w_out], [0], [(h, True)], [],
                          lambda accs, rows, cols: (rows[0] + accs[0],), [(F32, d)], tm, n_jo)[0]

        xn = _rmsnorm(h, ffn_norm[layer], BF16, f"ffn_norm_{layer}")
        buf = state_conv[layer].astype(F32)
        e1 = jnp.pad(buf[:, 1:2], ((0, 0), (0, dec_seq - 1), (0, 0))).reshape(ms, dff)
        e2 = jnp.pad(buf, ((0, 0), (0, dec_seq - 2), (0, 0))).reshape(ms, dff)
        n_jf = _pick_n_tiles([dff, dff], [d, d], d * 2, tm, 4 * 4)
        hid, tails, a_samp = _ffn_up(f"ffn_up_{layer}", xn, (ffn_wgate, layer), (ffn_wup, layer), ffn_conv_w[layer],
                                     ffn_conv_b[layer], e1, e2, n_prompt_tiles, tiles_per_seq, dec_seq, tm, n_jf)
        conv_p.append(tails[tiles_per_seq - 1:n_prompt_tiles:tiles_per_seq, SUBLANES - 2:, :].astype(state_conv.dtype))
        conv_s.append(a_samp.reshape(n_bd, dec_seq, dff)[:, dec_seq - 2:, :].astype(state_conv.dtype))
        tmd = math.gcd(tm, 256)
        n_jd = _pick_n_tiles([d], [dff], dff * 2, tmd, 2 * 4, bf16_weights=True)
        h = _fused_matmul(f"ffn_down_{layer}", [hid], [(wdown_bf, layer)], [0], [(h, True)], [],
                          lambda accs, rows, cols: (rows[0] + accs[0],), [(F32, d)], tmd, n_jd)[0]

        xn = _rmsnorm(h, ple_norm[layer], BF16, f"ple_norm_{layer}")
        n_jp = _pick_n_tiles([d, d], [d, ple_dim], (d + 2 * ple_dim) * 2, tm, 2 * 4)
        h = _fused_matmul(f"ple_{layer}", [xn, p_all[layer]], [(ple_wg, layer), (ple_we, layer)], [0, 1], [(h, True)], [],
                          lambda accs, rows, cols: (rows[0] + jax.nn.sigmoid(accs[0]) * accs[1],),
                          [(F32, d)], tm, n_jp)[0]

        if layer == n_a - 1:
            xk = _rmsnorm(h, kv_norm, BF16, "kv_norm")
            swap = jnp.concatenate([jnp.arange(half, rope_dim), jnp.arange(half)])
            w_pe = w_dkv[:, kv_lora:]

            def kvs_epilogue(accs, rows, cols):
                x = accs[0]
                c = x * lax.rsqrt(jnp.mean(x * x, axis=-1, keepdims=True) + EPS) * cols[0]
                return c, accs[1] * rows[0] + accs[2] * rows[1]

            c_all, kpe_all = _fused_matmul(
                "kv_stream", [xk], [(w_dkv[:, :kv_lora], None), (w_pe, None), (w_pe[:, swap], None)], [0, 0, 0],
                [(cos2, False), (sin2, False)], [ckv_norm.reshape(1, kv_lora).astype(F32)],
                kvs_epilogue, [(F32, kv_lora), (F32, rope_dim)], tm, 1)

    y_p = _rmsnorm(h, final_norm, x_prompt.dtype, "final_norm_prompt", 0, mp)
    y_s = _rmsnorm(h, final_norm, x_sample.dtype, "final_norm_sample", mp, ms)
    return (y_p.reshape(n_b, seq, d), y_s.reshape(n_bd, dec_seq, d),
            jnp.stack(states_p).astype(state_hgrn.dtype), jnp.stack(states_s),
            jnp.stack(conv_p), jnp.stack(conv_s),
            c_all[:mp].reshape(n_b, seq, kv_lora), c_all[mp:].reshape(n_bd, dec_seq, kv_lora),
            kpe_all[:mp].reshape(n_b, seq, rope_dim), kpe_all[mp:].reshape(n_bd, dec_seq, rope_dim))
```

```python
import math

import jax
import jax.numpy as jnp
from jax import lax
from jax.experimental import pallas as pl
from jax.experimental.pallas import tpu as pltpu

F32 = jnp.float32
BF16 = jnp.bfloat16
EPS = 1e-6
ROPE_THETA = 10000.0
HG_DK = 128
HG_CHUNK = 32
SUBLANES = 8
LANES = 128
BF16_ROWS = 16
VMEM_LIMIT = 56 * 1024 * 1024
VMEM_BUDGET = 46 * 1024 * 1024
PAGES_PER_STEP = 32
DECODE_CHAINS = 1
DECODE_SLOTS = 3
FFN_ROW_GROUPS = 1
MAX_ROW_TILE = 1152


def _params(sem):
    return pltpu.CompilerParams(dimension_semantics=sem, vmem_limit_bytes=VMEM_LIMIT)


def _cast_rows(w_ref, wbf_ref):
    k = w_ref.shape[0]
    ch = math.gcd(k, 512)

    def step(c, carry):
        r = pl.multiple_of(c * ch, ch)
        wbf_ref[pl.ds(r, ch), :] = w_ref[pl.ds(r, ch), :].astype(BF16)
        return carry

    lax.fori_loop(0, k // ch, step, 0)


def _dot(a, b):
    return jnp.dot(a, b, preferred_element_type=F32)


def _dot_nt(a, b):
    return lax.dot_general(a, b, (((1,), (1,)), ((), ())), preferred_element_type=F32)


def _dot_tn(a, b):
    return lax.dot_general(a, b, (((0,), (0,)), ((), ())), preferred_element_type=F32)


def _split3(x):
    p1 = x.astype(BF16).astype(F32)
    r1 = x - p1
    p2 = r1.astype(BF16).astype(F32)
    p3 = (r1 - p2).astype(BF16).astype(F32)
    return p1, p2, p3


def _weight_spec(w, n_j):
    arr, layer = w
    k, n = arr.shape[-2:]
    if arr.ndim == 3:
        return pl.BlockSpec((None, k, n // n_j), lambda j, i: (layer, 0, j))
    return pl.BlockSpec((k, n // n_j), lambda j, i: (0, j))


def _rmsnorm_body(h_ref, g_ref, o_ref):
    x = h_ref[...]
    inv = lax.rsqrt(jnp.mean(x * x, axis=-1, keepdims=True) + EPS)
    o_ref[...] = (x * inv * g_ref[...]).astype(o_ref.dtype)


def _rmsnorm(h, g, out_dtype, name, row0=0, n_rows=None):
    d = h.shape[1]
    m = h.shape[0] if n_rows is None else n_rows
    tr = math.gcd(math.gcd(m, row0), 256)
    blk0 = row0 // tr
    return pl.pallas_call(
        _rmsnorm_body,
        grid=(m // tr,),
        in_specs=[pl.BlockSpec((tr, d), lambda i: (i + blk0, 0)), pl.BlockSpec((1, d), lambda i: (0, 0))],
        out_specs=pl.BlockSpec((tr, d), lambda i: (i, 0)),
        out_shape=jax.ShapeDtypeStruct((m, d), out_dtype),
        compiler_params=_params(("arbitrary",)),
        name=name,
    )(h, g.reshape(1, d).astype(F32))


def _fused_matmul(name, xs, ws, w_x, row_ins, col_ins, epilogue, outs, tm, n_j, m_rows=None):
    m = xs[0].shape[0] if m_rows is None else m_rows
    n_i = m // tm
    nx, nw, nr, nc, no = len(xs), len(ws), len(row_ins), len(col_ins), len(outs)
    needs_copy = [w.dtype != BF16 for w, _ in ws]

    def body(*refs):
        x_refs = refs[:nx]
        w_refs = refs[nx:nx + nw]
        r_refs = refs[nx + nw:nx + nw + nr]
        c_refs = refs[nx + nw + nr:nx + nw + nr + nc]
        o_refs = refs[nx + nw + nr + nc:nx + nw + nr + nc + no]
        scratch_refs = list(refs[nx + nw + nr + nc + no:])
        wbf_refs = [scratch_refs.pop(0) if needs_copy[j] else w_refs[j] for j in range(nw)]

        @pl.when(pl.program_id(1) == 0)
        def _():
            for j in range(nw):
                if needs_copy[j]:
                    _cast_rows(w_refs[j], wbf_refs[j])

        xv = [x[...].astype(BF16) for x in x_refs]
        accs = [_dot(xv[w_x[j]], wbf_refs[j][...]) for j in range(nw)]
        res = epilogue(accs, [r[...] for r in r_refs], [c[...] for c in c_refs])
        for o_ref, r in zip(o_refs, res):
            o_ref[...] = r.astype(o_ref.dtype)

    in_specs = [pl.BlockSpec((tm, x.shape[1]), lambda j, i: (i, 0)) for x in xs]
    in_specs += [_weight_spec(w, n_j) for w in ws]
    for arr, tiled in row_ins:
        if tiled:
            in_specs.append(pl.BlockSpec((tm, arr.shape[1] // n_j), lambda j, i: (i, j)))
        else:
            in_specs.append(pl.BlockSpec((tm, arr.shape[1]), lambda j, i: (i, 0)))
    in_specs += [pl.BlockSpec((c.shape[0], c.shape[1] // n_j), lambda j, i: (0, j)) for c in col_ins]
    out_specs = [pl.BlockSpec((tm, w // n_j), lambda j, i: (i, j)) for _, w in outs]
    out_shape = [jax.ShapeDtypeStruct((m, w), dt) for dt, w in outs]
    scratch = [pltpu.VMEM((w.shape[-2], w.shape[-1] // n_j), BF16) for (w, _), c in zip(ws, needs_copy) if c]
    return pl.pallas_call(
        body,
        grid=(n_j, n_i),
        in_specs=in_specs,
        out_specs=out_specs,
        out_shape=out_shape,
        scratch_shapes=scratch,
        compiler_params=_params(("arbitrary", "arbitrary")),
        name=name,
    )(*xs, *[w for w, _ in ws], *[a for a, _ in row_ins], *col_ins)


def _pick_n_tiles(widths, ks, x_row_bytes, tm, io_col_bytes, align=LANES, bf16_weights=False):
    w_elem_bytes = 2 * 2 if bf16_weights else 2 * 4 + 2
    for n in range(1, max(widths) + 1):
        if any(w % n or (w // n) % align for w in widths):
            continue
        tns = [w // n for w in widths]
        wbytes = sum(k * t * w_elem_bytes for k, t in zip(ks, tns))
        xbytes = 2 * tm * x_row_bytes
        iobytes = 2 * tm * max(tns) * io_col_bytes
        if wbytes + xbytes + iobytes <= VMEM_BUDGET and max(tns) <= 1024:
            return n
    raise ValueError("no column tiling fits the VMEM budget")


def _pick_row_tile(m):
    for tm in range(min(m, MAX_ROW_TILE), 0, -1):
        if m % tm == 0 and tm % BF16_ROWS == 0:
            return tm
    raise ValueError("row count has no bf16-tile-aligned divisor")


def _ffn_up(name, xn, wg, wu, conv_w, conv_b, e1, e2, n_prompt_tiles, tiles_per_seq, dec_seq, tm, n_j):
    m, d = xn.shape
    dff = wg[0].shape[-1]
    tn = dff // n_j
    n_i = m // tm
    carry = SUBLANES

    def body(x_ref, wg_ref, wu_ref, cw_ref, cb_ref, e1_ref, e2_ref, hid_ref, tail_ref, asamp_ref,
             wg_bf, wu_bf, ext):
        i = pl.program_id(1)

        @pl.when(i == 0)
        def _():
            _cast_rows(wg_ref, wg_bf)
            _cast_rows(wu_ref, wu_bf)
            ext[0:carry, :] = jnp.zeros((carry, tn), F32)

        is_sample = i >= n_prompt_tiles
        fresh = jnp.logical_or(i % tiles_per_seq == 0, is_sample)
        ext[0:carry, :] = jnp.where(fresh, 0.0, ext[0:carry, :])
        cw = cw_ref[...]
        cb = cb_ref[...]
        tg = tm // FFN_ROW_GROUPS
        t = lax.broadcasted_iota(jnp.int32, (tg, tn), 0) % dec_seq
        for grp in range(FFN_ROW_GROUPS):
            r0 = grp * tg
            x = x_ref[r0:r0 + tg, :]
            a = _dot(x, wg_bf[...])
            u = _dot(x, wu_bf[...])
            ext[carry + r0:carry + r0 + tg, :] = a
            prev1 = jnp.where(jnp.logical_and(is_sample, t == 0), e1_ref[r0:r0 + tg, :],
                              ext[carry - 1 + r0:carry - 1 + r0 + tg, :])
            prev2 = jnp.where(jnp.logical_and(is_sample, t <= 1), e2_ref[r0:r0 + tg, :],
                              ext[carry - 2 + r0:carry - 2 + r0 + tg, :])
            conv = cb + prev2 * cw[0:1, :] + prev1 * cw[1:2, :] + a * cw[2:3, :]
            gelu = 0.5 * conv * (1.0 + lax.erf(conv * math.sqrt(0.5)))
            hid_ref[r0:r0 + tg, :] = (gelu * u).astype(hid_ref.dtype)
            asamp_ref[r0:r0 + tg, :] = a
        tail = ext[tm:tm + carry, :]
        ext[0:carry, :] = tail
        tail_ref[0] = tail

    return pl.pallas_call(
        body,
        grid=(n_j, n_i),
        in_specs=[
            pl.BlockSpec((tm, d), lambda j, i: (i, 0)),
            _weight_spec(wg, n_j),
            _weight_spec(wu, n_j),
            pl.BlockSpec((conv_w.shape[0], tn), lambda j, i: (0, j)),
            pl.BlockSpec((1, tn), lambda j, i: (0, j)),
            pl.BlockSpec((tm, tn), lambda j, i: (0, j)),
            pl.BlockSpec((tm, tn), lambda j, i: (0, j)),
        ],
        out_specs=[
            pl.BlockSpec((tm, tn), lambda j, i: (i, j)),
            pl.BlockSpec((1, carry, tn), lambda j, i: (i, 0, j)),
            pl.BlockSpec((tm, tn), lambda j, i: (0, j)),
        ],
        out_shape=[
            jax.ShapeDtypeStruct((m, dff), BF16),
            jax.ShapeDtypeStruct((n_i, carry, dff), F32),
            jax.ShapeDtypeStruct((tm, dff), F32),
        ],
        scratch_shapes=[pltpu.VMEM((d, tn), BF16), pltpu.VMEM((d, tn), BF16),
                        pltpu.VMEM((tm + carry, tn), F32)],
        compiler_params=_params(("arbitrary", "arbitrary")),
        name=name,
    )(xn, wg[0], wu[0], conv_w, conv_b.reshape(1, dff), e1, e2)


def _gla(name, q, f, v, gate, onorm, s0, n_seq, seq_len, chunk, n_par, n_ser, hb):
    hw = q.shape[1]
    n_heads = hw // HG_DK
    dk = HG_DK
    rows = n_par * n_ser * chunk
    nt = seq_len // (n_ser * chunk)
    assert n_par == 1 or nt == 1
    n_hg = n_heads // hb
    kp = -(-3 * chunk // BF16_ROWS) * BF16_ROWS
    def body(q_ref, f_ref, v_ref, g_ref, on_ref, s0_ref, o_ref, sf_ref, st_scr):
        t = pl.program_id(2)

        @pl.when(t == 0)
        def _():
            st_scr[...] = s0_ref[...]

        row = lax.broadcasted_iota(jnp.int32, (chunk, chunk), 0)
        col = lax.broadcasted_iota(jnp.int32, (chunk, chunk), 1)
        tril = row >= col
        rr = lax.broadcasted_iota(jnp.int32, (chunk, kp), 0)
        cc = lax.broadcasted_iota(jnp.int32, (chunk, kp), 1)
        tril3 = jnp.where(jnp.logical_and(cc % chunk <= rr, cc < 3 * chunk), 1.0, 0.0).astype(BF16)
        ones = jnp.ones((BF16_ROWS, dk), BF16)

        for par in range(n_par):
            for ser in range(n_ser):
                r0 = (par * n_ser + ser) * chunk
                fc = f_ref[r0:r0 + chunk, :]
                qc = q_ref[r0:r0 + chunk, :]
                vc = v_ref[r0:r0 + chunk, :].astype(BF16)
                parts = list(_split3(jnp.log(fc)))
                if kp > 3 * chunk:
                    parts.append(jnp.zeros((kp - 3 * chunk, hb * dk), F32))
                b = _dot(tril3, jnp.concatenate(parts, axis=0).astype(BF16))
                b_end = b[chunk - 1:chunk, :]
                kc = 1.0 - fc
                q_dec = (qc * jnp.exp(b)).astype(BF16)
                k_inc = (kc * jnp.exp(-b)).astype(BF16)
                k_dec = (kc * jnp.exp(b_end - b)).astype(BF16)
                end3 = jnp.concatenate(list(_split3(b_end)) + [jnp.zeros((BF16_ROWS - 3, hb * dk), F32)],
                                       axis=0).astype(BF16)
                for hd in range(hb):
                    ls = slice(hd * dk, (hd + 1) * dk)
                    e_col = jnp.exp(_dot_tn(end3[:, ls], ones))
                    a = jnp.where(tril, _dot_nt(q_dec[:, ls], k_inc[:, ls]), 0.0)
                    st = st_scr[par, hd]
                    o = _dot(q_dec[:, ls], st.astype(BF16)) + _dot(a.astype(BF16), vc[:, ls])
                    st_scr[par, hd] = st * e_col + _dot_tn(k_dec[:, ls], vc[:, ls])
                    y = o * lax.rsqrt(jnp.mean(o * o, axis=-1, keepdims=True) + EPS) * on_ref[:, ls]
                    o_ref[r0:r0 + chunk, ls] = (y * g_ref[r0:r0 + chunk, ls]).astype(o_ref.dtype)

        @pl.when(t == nt - 1)
        def _():
            sf_ref[...] = st_scr[...]

    row_spec = pl.BlockSpec((rows, hb * dk), lambda s, h, t: (s * nt + t, h))
    st_spec = pl.BlockSpec((n_par, hb, dk, dk), lambda s, h, t: (s, h, 0, 0))
    return pl.pallas_call(
        body,
        grid=(n_seq // n_par, n_hg, nt),
        in_specs=[row_spec, row_spec, row_spec, row_spec,
                  pl.BlockSpec((1, hb * dk), lambda s, h, t: (0, h)), st_spec],
        out_specs=[row_spec, st_spec],
        out_shape=[jax.ShapeDtypeStruct((n_seq * seq_len, hw), BF16),
                   jax.ShapeDtypeStruct((n_seq, n_heads, dk, dk), F32)],
        scratch_shapes=[pltpu.VMEM((n_par, hb, dk, dk), F32)],
        compiler_params=_params(("arbitrary", "arbitrary", "arbitrary")),
        name=name,
    )(q, f, v, gate, onorm.reshape(1, hw).astype(F32), s0)


def _flash_prompt(name, q, k, v, n_b, seq, n_heads, qk_w, v_w, bq, bk):
    nq = seq // bq
    r = bq // bk

    def body(q_ref, k_ref, v_ref, o_ref):
        row = lax.broadcasted_iota(jnp.int32, (bk, bk), 0)
        col = lax.broadcasted_iota(jnp.int32, (bk, bk), 1)
        causal = col <= row

        for qi in range(nq):
            m = [jnp.full((bk, 1), -jnp.inf, F32) for _ in range(r)]
            l = [jnp.zeros((bk, 1), F32) for _ in range(r)]
            acc = [jnp.zeros((bk, v_w), F32) for _ in range(r)]

            def sweep(d0, key0, masked):
                q_rows = q_ref[qi * bq + d0 * bk:(qi + 1) * bq, :]
                s = _dot_nt(q_rows, k_ref[key0:key0 + bk, :])
                ps, alphas = [], []
                for d in range(d0, r):
                    sd = s[(d - d0) * bk:(d - d0 + 1) * bk, :]
                    if masked and d == d0:
                        sd = jnp.where(causal, sd, -jnp.inf)
                    m_new = jnp.maximum(m[d], jnp.max(sd, axis=-1, keepdims=True))
                    alpha = jnp.exp(m[d] - m_new)
                    p = jnp.exp(sd - m_new)
                    l[d] = alpha * l[d] + jnp.sum(p, axis=-1, keepdims=True)
                    m[d] = m_new
                    ps.append(p.astype(BF16))
                    alphas.append(alpha)
                pv = _dot(jnp.concatenate(ps, axis=0), v_ref[key0:key0 + bk, :])
                for d in range(d0, r):
                    acc[d] = alphas[d - d0] * acc[d] + pv[(d - d0) * bk:(d - d0 + 1) * bk, :]

            for j in range(qi * r):
                sweep(0, j * bk, False)
            for d0 in range(r):
                sweep(d0, qi * bq + d0 * bk, True)
            for d in range(r):
                o_ref[qi * bq + d * bk:qi * bq + (d + 1) * bk, :] = (acc[d] / l[d]).astype(o_ref.dtype)

    return pl.pallas_call(
        body,
        grid=(n_b, n_heads),
        in_specs=[
            pl.BlockSpec((seq, qk_w), lambda b, h: (b, h)),
            pl.BlockSpec((seq, qk_w), lambda b, h: (b, h)),
            pl.BlockSpec((seq, v_w), lambda b, h: (b, h)),
        ],
        out_specs=pl.BlockSpec((seq, v_w), lambda b, h: (b, h)),
        out_shape=jax.ShapeDtypeStruct((n_b * seq, n_heads * v_w), BF16),
        compiler_params=_params(("arbitrary", "arbitrary")),
        name=name,
    )(q, k, v)


def _decode_attention(name, ql, qp, c_new, k_new, cache_ckv, cache_kpe_t, page_table, dec_seq):
    n_b, n_r, c_w = ql.shape
    p_w = qp.shape[2]
    page = cache_ckv.shape[1]
    n_pages = page_table.shape[1]
    g = math.gcd(n_pages, PAGES_PER_STEP)
    n_ch = math.gcd(g, DECODE_CHAINS)
    gc = g // n_ch
    n_steps = n_pages // g
    n_total = n_b * n_steps
    n_new = c_new.shape[1]
    look = DECODE_SLOTS - 1

    def body(pt_ref, ql_ref, qp_ref, cn_ref, kn_ref, ckv_hbm, kpe_hbm, o_ref,
             cbuf, kbuf, sem, m_scr, l_scr, acc_scr):
        t = pl.program_id(0)
        s_i = lax.rem(t, n_steps)
        slot = lax.rem(t, DECODE_SLOTS)

        def copies(step, slot_):
            cps = []
            for idx in range(g):
                pg = pt_ref[step * g + idx]
                cps.append(pltpu.make_async_copy(ckv_hbm.at[pg], cbuf.at[slot_, idx], sem.at[slot_, 0]))
                cps.append(pltpu.make_async_copy(kpe_hbm.at[pg], kbuf.at[slot_, idx], sem.at[slot_, 1]))
            return cps

        @pl.when(t == 0)
        def _():
            for ahead in range(look):
                for cp in copies(min(ahead, n_total - 1), ahead):
                    cp.start()

        @pl.when(s_i == 0)
        def _():
            m_scr[...] = jnp.full(m_scr.shape, -jnp.inf, F32)
            l_scr[...] = jnp.zeros(l_scr.shape, F32)
            acc_scr[...] = jnp.zeros(acc_scr.shape, F32)

        for cp in copies(t, slot):
            cp.wait()

        q_l = ql_ref[0]
        q_p = qp_ref[0]

        def update(ch, s, c_bf):
            m_prev = m_scr[ch]
            m_new = jnp.maximum(m_prev, jnp.max(s, axis=-1, keepdims=True))
            alpha = jnp.exp(m_prev - m_new)
            p = jnp.exp(s - m_new)
            l_scr[ch] = alpha * l_scr[ch] + jnp.sum(p, axis=-1, keepdims=True)
            acc_scr[ch] = alpha * acc_scr[ch] + _dot(p.astype(BF16), c_bf)
            m_scr[ch] = m_new

        for ch in range(n_ch):
            c_bf = jnp.concatenate([cbuf[slot, idx].astype(BF16) for idx in range(ch * gc, (ch + 1) * gc)], axis=0)
            kt_bf = jnp.concatenate([kbuf[slot, idx].astype(BF16) for idx in range(ch * gc, (ch + 1) * gc)], axis=1)
            update(ch, _dot_nt(q_l, c_bf) + _dot(q_p, kt_bf), c_bf)

        for cp in copies(jnp.minimum(t + look, n_total - 1), lax.rem(t + look, DECODE_SLOTS)):
            cp.start()

        @pl.when(s_i == n_steps - 1)
        def _():
            cn = cn_ref[0].astype(BF16)
            kn = kn_ref[0].astype(BF16)
            s = _dot_nt(q_l, cn) + _dot_nt(q_p, kn)
            tok = lax.broadcasted_iota(jnp.int32, (n_r, n_new), 0) % dec_seq
            key = lax.broadcasted_iota(jnp.int32, (n_r, n_new), 1)
            update(0, jnp.where(key <= tok, s, -jnp.inf), cn)
            m_all = m_scr[0]
            for ch in range(1, n_ch):
                m_all = jnp.maximum(m_all, m_scr[ch])
            l_all = jnp.zeros_like(m_all)
            acc_all = jnp.zeros((n_r, c_w), F32)
            for ch in range(n_ch):
                w = jnp.exp(m_scr[ch] - m_all)
                l_all = l_all + w * l_scr[ch]
                acc_all = acc_all + w * acc_scr[ch]
            o_ref[0] = acc_all / l_all

        @pl.when(t == n_total - 1)
        def _():
            for ahead in range(1, DECODE_SLOTS):
                for cp in copies(n_total - 1, lax.rem(t + ahead, DECODE_SLOTS)):
                    cp.wait()

    grid_spec = pltpu.PrefetchScalarGridSpec(
        num_scalar_prefetch=1,
        grid=(n_total,),
        in_specs=[
            pl.BlockSpec((1, n_r, c_w), lambda t, pt: (t // n_steps, 0, 0)),
            pl.BlockSpec((1, n_r, p_w), lambda t, pt: (t // n_steps, 0, 0)),
            pl.BlockSpec((1, n_new, c_w), lambda t, pt: (t // n_steps, 0, 0)),
            pl.BlockSpec((1, n_new, p_w), lambda t, pt: (t // n_steps, 0, 0)),
            pl.BlockSpec(memory_space=pl.ANY),
            pl.BlockSpec(memory_space=pl.ANY),
        ],
        out_specs=pl.BlockSpec((1, n_r, c_w), lambda t, pt: (t // n_steps, 0, 0)),
        scratch_shapes=[pltpu.VMEM((DECODE_SLOTS, g, page, c_w), F32),
                        pltpu.VMEM((DECODE_SLOTS, g, p_w, page), F32),
                        pltpu.SemaphoreType.DMA((DECODE_SLOTS, 2)),
                        pltpu.VMEM((n_ch, n_r, 1), F32), pltpu.VMEM((n_ch, n_r, 1), F32),
                        pltpu.VMEM((n_ch, n_r, c_w), F32)],
    )
    return pl.pallas_call(
        body,
        grid_spec=grid_spec,
        out_shape=jax.ShapeDtypeStruct((n_b, n_r, c_w), F32),
        compiler_params=_params(("arbitrary",)),
        name=name,
    )(page_table.reshape(-1), ql, qp, c_new, k_new, cache_ckv, cache_kpe_t)


def _head_matmul(name, x, w, out_dtype):
    n_h, m, k = x.shape
    n = w.shape[2]

    def body(x_ref, w_ref, o_ref):
        o_ref[0] = _dot(x_ref[0].astype(BF16), w_ref[0].astype(BF16)).astype(o_ref.dtype)

    return pl.pallas_call(
        body,
        grid=(n_h,),
        in_specs=[pl.BlockSpec((1, m, k), lambda h: (h, 0, 0)), pl.BlockSpec((1, k, n), lambda h: (h, 0, 0))],
        out_specs=pl.BlockSpec((1, m, n), lambda h: (h, 0, 0)),
        out_shape=jax.ShapeDtypeStruct((n_h, m, n), out_dtype),
        compiler_params=_params(("arbitrary",)),
        name=name,
    )(x, w)


def kernel(x_prompt, x_sample, state_hgrn, state_conv, cache_ckv, cache_kpe, page_table, p_prompt, p_sample, attn_norm, ffn_norm, ple_norm, final_norm, hg_wq, hg_wf, hg_wi, hg_wg, hg_onorm, hg_wo, hg_lb_raw, kv_norm, w_dkv, ckv_norm, w_uk, w_uv, q_wd, q_norm, q_wu, mla_wo, ffn_wgate, ffn_wup, ffn_conv_w, ffn_conv_b, ffn_wdown, ple_wg, ple_we):
    n_b, seq, d = x_prompt.shape
    n_bd, dec_seq, _ = x_sample.shape
    depth = attn_norm.shape[0]
    n_a = hg_wq.shape[0]
    dff = ffn_wgate.shape[2]
    ple_dim = p_prompt.shape[3]
    kv_lora, mla_heads, nope = w_uk.shape
    v_dim = w_uv.shape[2]
    rope_dim = w_dkv.shape[1] - kv_lora
    half = rope_dim // 2
    q_lora = q_wd.shape[2]
    page = cache_ckv.shape[1]
    past_len = page_table.shape[1] * page
    hg_heads = hg_wq.shape[2] // HG_DK
    scale = float(nope + rope_dim) ** -0.5
    qk_w = 2 * LANES
    assert nope == LANES and v_dim == LANES and rope_dim <= LANES and rope_dim % 2 == 0

    mp, ms = n_b * seq, n_bd * dec_seq
    m = mp + ms
    tm = ms
    assert tm % BF16_ROWS == 0 and seq % tm == 0 and 2 <= dec_seq <= SUBLANES
    n_prompt_tiles = mp // tm
    tiles_per_seq = seq // tm
    tmt = _pick_row_tile(m)

    h = jnp.concatenate([x_prompt.reshape(mp, d), x_sample.reshape(ms, d)], axis=0)
    p_all = jnp.concatenate([p_prompt.reshape(depth, mp, ple_dim), p_sample.reshape(depth, ms, ple_dim)], axis=1)

    pos = jnp.concatenate([jnp.tile(jnp.arange(seq, dtype=jnp.int32), n_b),
                           jnp.tile(past_len + jnp.arange(dec_seq, dtype=jnp.int32), n_bd)])
    inv = ROPE_THETA ** (-jnp.arange(half, dtype=F32) / half)
    ang = pos.astype(F32)[:, None] * inv[None, :]
    cos2 = jnp.concatenate([jnp.cos(ang), jnp.cos(ang)], axis=1)
    sin2 = jnp.concatenate([-jnp.sin(ang), jnp.sin(ang)], axis=1)
    cos_pad = jnp.pad(cos2, ((0, 0), (0, LANES - rope_dim)))
    sin_pad = jnp.pad(sin2, ((0, 0), (0, LANES - rope_dim)))

    wdown_bf = ffn_wdown.astype(BF16)

    states_p, states_s, conv_p, conv_s = [], [], [], []
    c_all = kpe_all = None

    for layer in range(depth):
        xn = _rmsnorm(h, attn_norm[layer], BF16, f"attn_norm_{layer}")
        if layer < n_a:
            hw = hg_heads * HG_DK

            def qf_epilogue(accs, rows, cols, layer=layer):
                raw = cols[0]
                e = jnp.exp(raw - jnp.max(raw, axis=0, keepdims=True))
                lb = jnp.sum(e[:layer + 1], axis=0, keepdims=True) / jnp.sum(e, axis=0, keepdims=True)
                return jax.nn.silu(accs[0]), lb + (1.0 - lb) * jax.nn.sigmoid(accs[1])

            n_j = _pick_n_tiles([hw, hw], [d, d], d * 2, tmt, 2 * 4)
            q_act, f_act = _fused_matmul(
                f"hgrn_qf_{layer}", [xn], [(hg_wq, layer), (hg_wf, layer)], [0, 0], [], [hg_lb_raw.astype(F32)],
                qf_epilogue, [(F32, hw), (F32, hw)], tmt, n_j)
            v_act, g_act = _fused_matmul(
                f"hgrn_ig_{layer}", [xn], [(hg_wi, layer), (hg_wg, layer)], [0, 0], [], [],
                lambda accs, rows, cols: (accs[0], jax.nn.silu(accs[1])), [(F32, hw), (F32, hw)], tmt, n_j)

            hb = math.gcd(hg_heads, 16)
            n_ser = math.gcd(seq // HG_CHUNK, 4)
            o_p, s_p = _gla(f"gla_prompt_{layer}", q_act, f_act, v_act, g_act, hg_onorm[layer],
                            jnp.zeros((n_b, hg_heads, HG_DK, HG_DK), F32), n_b, seq, HG_CHUNK, 1, n_ser, hb)

            def pad_seq(a, fill):
                a = a[mp:].reshape(n_bd, dec_seq, hw)
                a = jnp.pad(a, ((0, 0), (0, SUBLANES - dec_seq), (0, 0)), constant_values=fill)
                return a.reshape(n_bd * SUBLANES, hw)

            o_s, s_s = _gla(f"gla_sample_{layer}", pad_seq(q_act, 0.0), pad_seq(f_act, 1.0), pad_seq(v_act, 0.0),
                            pad_seq(g_act, 0.0), hg_onorm[layer], state_hgrn[layer].astype(F32),
                            n_bd, SUBLANES, SUBLANES, math.gcd(n_bd, 4), 1, hb)
            o_s = o_s.reshape(n_bd, SUBLANES, hw)[:, :dec_seq].reshape(ms, hw)
            o_all = jnp.concatenate([o_p, o_s], axis=0)
            states_p.append(s_p)
            states_s.append(s_s.astype(state_hgrn.dtype))
            w_out = (hg_wo, layer)
        else:
            j = layer - n_a
            tmq = math.gcd(tm, 256)
            n_jq = _pick_n_tiles([q_lora], [d], d * 2, tmq, 2, align=q_lora)

            def cq_epilogue(accs, rows, cols):
                x = accs[0]
                return (x * lax.rsqrt(jnp.mean(x * x, axis=-1, keepdims=True) + EPS) * cols[0],)

            cq = _fused_matmul(f"mla_qdown_{j}", [xn], [(q_wd, j)], [0], [], [q_norm[j].reshape(1, q_lora).astype(F32)],
                               cq_epilogue, [(BF16, q_lora)], tmq, n_jq)[0]

            wq3 = q_wu[j].reshape(q_lora, mla_heads, nope + rope_dim)
            wq_pad = jnp.pad(wq3, ((0, 0), (0, 0), (0, qk_w - nope - rope_dim))).reshape(q_lora, mla_heads * qk_w)
            n_jh = _pick_n_tiles([mla_heads * qk_w], [q_lora], q_lora * 2, tmt, 2, align=qk_w)
            heads_per_tile = mla_heads // n_jh

            def q_epilogue(accs, rows, cols):
                cos_t, sin_t = rows
                lane = lax.broadcasted_iota(jnp.int32, cos_t.shape, 1)
                outs = []
                for hh in range(heads_per_tile):
                    lo = accs[0][:, hh * qk_w:hh * qk_w + LANES]
                    hi = accs[0][:, hh * qk_w + LANES:(hh + 1) * qk_w]
                    swapped = jnp.where(lane < half, pltpu.roll(hi, LANES - half, 1), pltpu.roll(hi, half, 1))
                    outs += [lo * scale, (hi * cos_t + swapped * sin_t) * scale]
                return (jnp.concatenate(outs, axis=1),)

            qfull = _fused_matmul(f"mla_qup_{j}", [cq], [(wq_pad, None)], [0], [(cos_pad, False), (sin_pad, False)], [],
                                  q_epilogue, [(BF16, mla_heads * qk_w)], tmt, n_jh)[0]

            kpe_pad = jnp.pad(kpe_all[:mp], ((0, 0), (0, LANES - rope_dim)))
            heads_k = math.gcd(mla_heads, 8)
            n_jk = mla_heads // heads_k

            def kv_epilogue(accs, rows, cols):
                ks = []
                for hh in range(heads_k):
                    ks += [accs[0][:, hh * nope:(hh + 1) * nope], rows[0]]
                return jnp.concatenate(ks, axis=1), accs[1]

            tmk = math.gcd(mp, 1024)
            kfull, vfull = _fused_matmul(
                f"mla_kv_up_{j}", [c_all],
                [(w_uk.reshape(kv_lora, mla_heads * nope), None), (w_uv.reshape(kv_lora, mla_heads * v_dim), None)],
                [0, 0], [(kpe_pad, False)], [], kv_epilogue,
                [(BF16, mla_heads * qk_w), (BF16, mla_heads * v_dim)], tmk, n_jk, m_rows=mp)
            bq = math.gcd(seq, 512)
            o_p = _flash_prompt(f"mla_prompt_attn_{j}", qfull, kfull, vfull, n_b, seq, mla_heads, qk_w, v_dim,
                                bq, math.gcd(bq, 256))

            q_s = qfull[mp:].reshape(n_bd, dec_seq, mla_heads, qk_w)
            q_nope_h = q_s[..., :nope].transpose(2, 0, 1, 3).reshape(mla_heads, ms, nope)
            q_lat = _head_matmul(f"mla_q_absorb_{j}", q_nope_h, w_uk.transpose(1, 2, 0), BF16)
            ql = q_lat.reshape(mla_heads, n_bd, dec_seq, kv_lora).transpose(1, 0, 2, 3).reshape(n_bd, mla_heads * dec_seq, kv_lora)
            qp = q_s[..., nope:nope + rope_dim].transpose(0, 2, 1, 3).reshape(n_bd, mla_heads * dec_seq, rope_dim)
            pad_new = ((0, 0), (0, BF16_ROWS - dec_seq), (0, 0))
            c_new = jnp.pad(c_all[mp:].reshape(n_bd, dec_seq, kv_lora), pad_new)
            k_new = jnp.pad(kpe_all[mp:].reshape(n_bd, dec_seq, rope_dim), pad_new)
            lat = _decode_attention(f"mla_sample_attn_{j}", ql, qp, c_new, k_new, cache_ckv,
                                    jnp.transpose(cache_kpe, (0, 2, 1)), page_table.astype(jnp.int32), dec_seq)
            lat_h = lat.reshape(n_bd, mla_heads, dec_seq, kv_lora).transpose(1, 0, 2, 3).reshape(mla_heads, ms, kv_lora)
            o_sh = _head_matmul(f"mla_v_up_{j}", lat_h, w_uv.transpose(1, 0, 2), BF16)
            o_s = o_sh.transpose(1, 0, 2).reshape(ms, mla_heads * v_dim)
            o_all = jnp.concatenate([o_p, o_s], axis=0)
            w_out = (mla_wo, j)

        k_out = w_out[0].shape[-2]
        n_jo = _pick_n_tiles([d], [k_out], k_out * 2, tm, 2 * 4)
        h = _fused_matmul(f"mixer_out_{layer}", [o_all], [w_out], [0], [(h, True)], [],
                          lambda accs, rows, cols: (rows[0] + accs[0],), [(F32, d)], tm, n_jo)[0]

        xn = _rmsnorm(h, ffn_norm[layer], BF16, f"ffn_norm_{layer}")
        buf = state_conv[layer].astype(F32)
        e1 = jnp.pad(buf[:, 1:2], ((0, 0), (0, dec_seq - 1), (0, 0))).reshape(ms, dff)
        e2 = jnp.pad(buf, ((0, 0), (0, dec_seq - 2), (0, 0))).reshape(ms, dff)
        n_jf = _pick_n_tiles([dff, dff], [d, d], d * 2, tm, 4 * 4)
        hid, tails, a_samp = _ffn_up(f"ffn_up_{layer}", xn, (ffn_wgate, layer), (ffn_wup, layer), ffn_conv_w[layer],
                                     ffn_conv_b[layer], e1, e2, n_prompt_tiles, tiles_per_seq, dec_seq, tm, n_jf)
        conv_p.append(tails[tiles_per_seq - 1:n_prompt_tiles:tiles_per_seq, SUBLANES - 2:, :].astype(state_conv.dtype))
        conv_s.append(a_samp.reshape(n_bd, dec_seq, dff)[:, dec_seq - 2:, :].astype(state_conv.dtype))
        tmd = math.gcd(tm, 256)
        n_jd = _pick_n_tiles([d], [dff], dff * 2, tmd, 2 * 4, bf16_weights=True)
        h = _fused_matmul(f"ffn_down_{layer}", [hid], [(wdown_bf, layer)], [0], [(h, True)], [],
                          lambda accs, rows, cols: (rows[0] + accs[0],), [(F32, d)], tmd, n_jd)[0]

        xn = _rmsnorm(h, ple_norm[layer], BF16, f"ple_norm_{layer}")
        n_jp = _pick_n_tiles([d, d], [d, ple_dim], (d + 2 * ple_dim) * 2, tm, 2 * 4)
        h = _fused_matmul(f"ple_{layer}", [xn, p_all[layer]], [(ple_wg, layer), (ple_we, layer)], [0, 1], [(h, True)], [],
                          lambda accs, rows, cols: (rows[0] + jax.nn.sigmoid(accs[0]) * accs[1],),
                          [(F32, d)], tm, n_jp)[0]

        if layer == n_a - 1:
            xk = _rmsnorm(h, kv_norm, BF16, "kv_norm")
            swap = jnp.concatenate([jnp.arange(half, rope_dim), jnp.arange(half)])
            w_pe = w_dkv[:, kv_lora:]

            def kvs_epilogue(accs, rows, cols):
                x = accs[0]
                c = x * lax.rsqrt(jnp.mean(x * x, axis=-1, keepdims=True) + EPS) * cols[0]
                return c, accs[1] * rows[0] + accs[2] * rows[1]

            c_all, kpe_all = _fused_matmul(
                "kv_stream", [xk], [(w_dkv[:, :kv_lora], None), (w_pe, None), (w_pe[:, swap], None)], [0, 0, 0],
                [(cos2, False), (sin2, False)], [ckv_norm.reshape(1, kv_lora).astype(F32)],
                kvs_epilogue, [(F32, kv_lora), (F32, rope_dim)], tm, 1)

    y_p = _rmsnorm(h, final_norm, x_prompt.dtype, "final_norm_prompt", 0, mp)
    y_s = _rmsnorm(h, final_norm, x_sample.dtype, "final_norm_sample", mp, ms)
    return (y_p.reshape(n_b, seq, d), y_s.reshape(n_bd, dec_seq, d),
            jnp.stack(states_p).astype(state_hgrn.dtype), jnp.stack(states_s),
            jnp.stack(conv_p), jnp.stack(conv_s),
            c_all[:mp].reshape(n_b, seq, kv_lora), c_all[mp:].reshape(n_bd, dec_seq, kv_lora),
            kpe_all[:mp].reshape(n_b, seq, rope_dim), kpe_all[mp:].reshape(n_bd, dec_seq, rope_dim))
```

```python
import math

import jax
import jax.numpy as jnp
from jax import lax
from jax.experimental import pallas as pl
from jax.experimental.pallas import tpu as pltpu

F32 = jnp.float32
BF16 = jnp.bfloat16
EPS = 1e-6
ROPE_THETA = 10000.0
HG_DK = 128
HG_CHUNK = 32
SUBLANES = 8
LANES = 128
BF16_ROWS = 16
VMEM_LIMIT = 56 * 1024 * 1024
VMEM_BUDGET = 48 * 1024 * 1024
PAGES_PER_STEP = 32
DECODE_CHAINS = 1
DECODE_SLOTS = 3
FFN_ROW_GROUPS = 1
MAX_ROW_TILE = 1152


def _params(sem):
    return pltpu.CompilerParams(dimension_semantics=sem, vmem_limit_bytes=VMEM_LIMIT)


def _cast_rows(w_ref, wbf_ref):
    k = w_ref.shape[0]
    ch = math.gcd(k, 512)

    def step(c, carry):
        r = pl.multiple_of(c * ch, ch)
        wbf_ref[pl.ds(r, ch), :] = w_ref[pl.ds(r, ch), :].astype(BF16)
        return carry

    lax.fori_loop(0, k // ch, step, 0)


def _dot(a, b):
    return jnp.dot(a, b, preferred_element_type=F32)


def _dot_nt(a, b):
    return lax.dot_general(a, b, (((1,), (1,)), ((), ())), preferred_element_type=F32)


def _dot_tn(a, b):
    return lax.dot_general(a, b, (((0,), (0,)), ((), ())), preferred_element_type=F32)


def _split3(x):
    p1 = x.astype(BF16).astype(F32)
    r1 = x - p1
    p2 = r1.astype(BF16).astype(F32)
    p3 = (r1 - p2).astype(BF16).astype(F32)
    return p1, p2, p3


def _weight_spec(w, n_j):
    arr, layer = w
    k, n = arr.shape[-2:]
    if arr.ndim == 3:
        return pl.BlockSpec((None, k, n // n_j), lambda j, i: (layer, 0, j))
    return pl.BlockSpec((k, n // n_j), lambda j, i: (0, j))


def _rmsnorm_body(h_ref, g_ref, o_ref):
    x = h_ref[...]
    inv = lax.rsqrt(jnp.mean(x * x, axis=-1, keepdims=True) + EPS)
    o_ref[...] = (x * inv * g_ref[...]).astype(o_ref.dtype)


def _rmsnorm(h, g, out_dtype, name, row0=0, n_rows=None):
    d = h.shape[1]
    m = h.shape[0] if n_rows is None else n_rows
    tr = math.gcd(math.gcd(m, row0), 512)
    blk0 = row0 // tr
    return pl.pallas_call(
        _rmsnorm_body,
        grid=(m // tr,),
        in_specs=[pl.BlockSpec((tr, d), lambda i: (i + blk0, 0)), pl.BlockSpec((1, d), lambda i: (0, 0))],
        out_specs=pl.BlockSpec((tr, d), lambda i: (i, 0)),
        out_shape=jax.ShapeDtypeStruct((m, d), out_dtype),
        compiler_params=_params(("arbitrary",)),
        name=name,
    )(h, g.reshape(1, d).astype(F32))


def _fused_matmul(name, xs, ws, w_x, row_ins, col_ins, epilogue, outs, tm, n_j, m_rows=None, norm_gain=None):
    m = xs[0].shape[0] if m_rows is None else m_rows
    n_i = m // tm
    nx, nw, nr, nc, no = len(xs), len(ws), len(row_ins), len(col_ins), len(outs)
    needs_copy = [w.dtype != BF16 for w, _ in ws]
    gains = [] if norm_gain is None else [norm_gain.reshape(1, -1).astype(F32)]
    ng = len(gains)

    def body(*refs):
        x_refs = refs[:nx]
        w_refs = refs[nx:nx + nw]
        r_refs = refs[nx + nw:nx + nw + nr]
        c_refs = refs[nx + nw + nr:nx + nw + nr + nc]
        g_refs = refs[nx + nw + nr + nc:nx + nw + nr + nc + ng]
        o_refs = refs[nx + nw + nr + nc + ng:nx + nw + nr + nc + ng + no]
        scratch_refs = list(refs[nx + nw + nr + nc + ng + no:])
        wbf_refs = [scratch_refs.pop(0) if needs_copy[j] else w_refs[j] for j in range(nw)]

        @pl.when(pl.program_id(1) == 0)
        def _():
            for j in range(nw):
                if needs_copy[j]:
                    _cast_rows(w_refs[j], wbf_refs[j])

        xv = [x[...] for x in x_refs]
        if ng:
            x0 = xv[0]
            xv[0] = x0 * lax.rsqrt(jnp.mean(x0 * x0, axis=-1, keepdims=True) + EPS) * g_refs[0][...]
        xv = [x.astype(BF16) for x in xv]
        accs = [_dot(xv[w_x[j]], wbf_refs[j][...]) for j in range(nw)]
        res = epilogue(accs, [r[...] for r in r_refs], [c[...] for c in c_refs])
        for o_ref, r in zip(o_refs, res):
            o_ref[...] = r.astype(o_ref.dtype)

    in_specs = [pl.BlockSpec((tm, x.shape[1]), lambda j, i: (i, 0)) for x in xs]
    in_specs += [_weight_spec(w, n_j) for w in ws]
    for arr, tiled in row_ins:
        if tiled:
            in_specs.append(pl.BlockSpec((tm, arr.shape[1] // n_j), lambda j, i: (i, j)))
        else:
            in_specs.append(pl.BlockSpec((tm, arr.shape[1]), lambda j, i: (i, 0)))
    in_specs += [pl.BlockSpec((c.shape[0], c.shape[1] // n_j), lambda j, i: (0, j)) for c in col_ins]
    in_specs += [pl.BlockSpec(g.shape, lambda j, i: (0, 0)) for g in gains]
    out_specs = [pl.BlockSpec((tm, w // n_j), lambda j, i: (i, j)) for _, w in outs]
    out_shape = [jax.ShapeDtypeStruct((m, w), dt) for dt, w in outs]
    scratch = [pltpu.VMEM((w.shape[-2], w.shape[-1] // n_j), BF16) for (w, _), c in zip(ws, needs_copy) if c]
    return pl.pallas_call(
        body,
        grid=(n_j, n_i),
        in_specs=in_specs,
        out_specs=out_specs,
        out_shape=out_shape,
        scratch_shapes=scratch,
        compiler_params=_params(("arbitrary", "arbitrary")),
        name=name,
    )(*xs, *[w for w, _ in ws], *[a for a, _ in row_ins], *col_ins, *gains)


def _pick_n_tiles(widths, ks, x_row_bytes, tm, io_col_bytes, align=LANES, bf16_weights=False):
    w_elem_bytes = 2 * 2 if bf16_weights else 2 * 4 + 2
    for n in range(1, max(widths) + 1):
        if any(w % n or (w // n) % align for w in widths):
            continue
        tns = [w // n for w in widths]
        wbytes = sum(k * t * w_elem_bytes for k, t in zip(ks, tns))
        xbytes = 2 * tm * x_row_bytes
        iobytes = 2 * tm * max(tns) * io_col_bytes
        if wbytes + xbytes + iobytes <= VMEM_BUDGET and max(tns) <= 1024:
            return n
    raise ValueError("no column tiling fits the VMEM budget")


def _pick_row_tile(m):
    for tm in range(min(m, MAX_ROW_TILE), 0, -1):
        if m % tm == 0 and tm % BF16_ROWS == 0:
            return tm
    raise ValueError("row count has no bf16-tile-aligned divisor")


def _ffn_up(name, xn, wg, wu, conv_w, conv_b, e1, e2, n_prompt_tiles, tiles_per_seq, dec_seq, tm, n_j):
    m, d = xn.shape
    dff = wg[0].shape[-1]
    tn = dff // n_j
    n_i = m // tm
    carry = SUBLANES

    def body(x_ref, wg_ref, wu_ref, cw_ref, cb_ref, e1_ref, e2_ref, hid_ref, tail_ref, asamp_ref,
             wg_bf, wu_bf, ext):
        i = pl.program_id(1)

        @pl.when(i == 0)
        def _():
            _cast_rows(wg_ref, wg_bf)
            _cast_rows(wu_ref, wu_bf)
            ext[0:carry, :] = jnp.zeros((carry, tn), F32)

        is_sample = i >= n_prompt_tiles
        fresh = jnp.logical_or(i % tiles_per_seq == 0, is_sample)
        ext[0:carry, :] = jnp.where(fresh, 0.0, ext[0:carry, :])
        cw = cw_ref[...]
        cb = cb_ref[...]
        tg = tm // FFN_ROW_GROUPS
        t = lax.broadcasted_iota(jnp.int32, (tg, tn), 0) % dec_seq
        for grp in range(FFN_ROW_GROUPS):
            r0 = grp * tg
            x = x_ref[r0:r0 + tg, :]
            a = _dot(x, wg_bf[...])
            u = _dot(x, wu_bf[...])
            ext[carry + r0:carry + r0 + tg, :] = a
            prev1 = jnp.where(jnp.logical_and(is_sample, t == 0), e1_ref[r0:r0 + tg, :],
                              ext[carry - 1 + r0:carry - 1 + r0 + tg, :])
            prev2 = jnp.where(jnp.logical_and(is_sample, t <= 1), e2_ref[r0:r0 + tg, :],
                              ext[carry - 2 + r0:carry - 2 + r0 + tg, :])
            conv = cb + prev2 * cw[0:1, :] + prev1 * cw[1:2, :] + a * cw[2:3, :]
            gelu = 0.5 * conv * (1.0 + lax.erf(conv * math.sqrt(0.5)))
            hid_ref[r0:r0 + tg, :] = (gelu * u).astype(hid_ref.dtype)
            asamp_ref[r0:r0 + tg, :] = a
        tail = ext[tm:tm + carry, :]
        ext[0:carry, :] = tail
        tail_ref[0] = tail

    return pl.pallas_call(
        body,
        grid=(n_j, n_i),
        in_specs=[
            pl.BlockSpec((tm, d), lambda j, i: (i, 0)),
            _weight_spec(wg, n_j),
            _weight_spec(wu, n_j),
            pl.BlockSpec((conv_w.shape[0], tn), lambda j, i: (0, j)),
            pl.BlockSpec((1, tn), lambda j, i: (0, j)),
            pl.BlockSpec((tm, tn), lambda j, i: (0, j)),
            pl.BlockSpec((tm, tn), lambda j, i: (0, j)),
        ],
        out_specs=[
            pl.BlockSpec((tm, tn), lambda j, i: (i, j)),
            pl.BlockSpec((1, carry, tn), lambda j, i: (i, 0, j)),
            pl.BlockSpec((tm, tn), lambda j, i: (0, j)),
        ],
        out_shape=[
            jax.ShapeDtypeStruct((m, dff), BF16),
            jax.ShapeDtypeStruct((n_i, carry, dff), F32),
            jax.ShapeDtypeStruct((tm, dff), F32),
        ],
        scratch_shapes=[pltpu.VMEM((d, tn), BF16), pltpu.VMEM((d, tn), BF16),
                        pltpu.VMEM((tm + carry, tn), F32)],
        compiler_params=_params(("arbitrary", "arbitrary")),
        name=name,
    )(xn, wg[0], wu[0], conv_w, conv_b.reshape(1, dff), e1, e2)


def _gla(name, q, f, v, gate, onorm, s0, n_seq, seq_len, chunk, n_par, n_ser, hb):
    hw = q.shape[1]
    n_heads = hw // HG_DK
    dk = HG_DK
    rows = n_par * n_ser * chunk
    nt = seq_len // (n_ser * chunk)
    assert n_par == 1 or nt == 1
    n_hg = n_heads // hb
    kp = -(-3 * chunk // BF16_ROWS) * BF16_ROWS
    def body(q_ref, f_ref, v_ref, g_ref, on_ref, s0_ref, o_ref, sf_ref, st_scr):
        t = pl.program_id(2)

        @pl.when(t == 0)
        def _():
            st_scr[...] = s0_ref[...]

        row = lax.broadcasted_iota(jnp.int32, (chunk, chunk), 0)
        col = lax.broadcasted_iota(jnp.int32, (chunk, chunk), 1)
        tril = row >= col
        rr = lax.broadcasted_iota(jnp.int32, (chunk, kp), 0)
        cc = lax.broadcasted_iota(jnp.int32, (chunk, kp), 1)
        tril3 = jnp.where(jnp.logical_and(cc % chunk <= rr, cc < 3 * chunk), 1.0, 0.0).astype(BF16)
        ones = jnp.ones((BF16_ROWS, dk), BF16)

        for par in range(n_par):
            for ser in range(n_ser):
                r0 = (par * n_ser + ser) * chunk
                fc = f_ref[r0:r0 + chunk, :]
                qc = q_ref[r0:r0 + chunk, :]
                vc = v_ref[r0:r0 + chunk, :].astype(BF16)
                parts = list(_split3(jnp.log(fc)))
                if kp > 3 * chunk:
                    parts.append(jnp.zeros((kp - 3 * chunk, hb * dk), F32))
                b = _dot(tril3, jnp.concatenate(parts, axis=0).astype(BF16))
                b_end = b[chunk - 1:chunk, :]
                kc = 1.0 - fc
                q_dec = (qc * jnp.exp(b)).astype(BF16)
                k_inc = (kc * jnp.exp(-b)).astype(BF16)
                k_dec = (kc * jnp.exp(b_end - b)).astype(BF16)
                end3 = jnp.concatenate(list(_split3(b_end)) + [jnp.zeros((BF16_ROWS - 3, hb * dk), F32)],
                                       axis=0).astype(BF16)
                for hd in range(hb):
                    ls = slice(hd * dk, (hd + 1) * dk)
                    e_col = jnp.exp(_dot_tn(end3[:, ls], ones))
                    a = jnp.where(tril, _dot_nt(q_dec[:, ls], k_inc[:, ls]), 0.0)
                    st = st_scr[par, hd]
                    o = _dot(q_dec[:, ls], st.astype(BF16)) + _dot(a.astype(BF16), vc[:, ls])
                    st_scr[par, hd] = st * e_col + _dot_tn(k_dec[:, ls], vc[:, ls])
                    y = o * lax.rsqrt(jnp.mean(o * o, axis=-1, keepdims=True) + EPS) * on_ref[:, ls]
                    o_ref[r0:r0 + chunk, ls] = (y * g_ref[r0:r0 + chunk, ls]).astype(o_ref.dtype)

        @pl.when(t == nt - 1)
        def _():
            sf_ref[...] = st_scr[...]

    row_spec = pl.BlockSpec((rows, hb * dk), lambda s, h, t: (s * nt + t, h))
    st_spec = pl.BlockSpec((n_par, hb, dk, dk), lambda s, h, t: (s, h, 0, 0))
    return pl.pallas_call(
        body,
        grid=(n_seq // n_par, n_hg, nt),
        in_specs=[row_spec, row_spec, row_spec, row_spec,
                  pl.BlockSpec((1, hb * dk), lambda s, h, t: (0, h)), st_spec],
        out_specs=[row_spec, st_spec],
        out_shape=[jax.ShapeDtypeStruct((n_seq * seq_len, hw), BF16),
                   jax.ShapeDtypeStruct((n_seq, n_heads, dk, dk), F32)],
        scratch_shapes=[pltpu.VMEM((n_par, hb, dk, dk), F32)],
        compiler_params=_params(("arbitrary", "arbitrary", "arbitrary")),
        name=name,
    )(q, f, v, gate, onorm.reshape(1, hw).astype(F32), s0)


def _flash_prompt(name, q, k, v, n_b, seq, n_heads, qk_w, v_w, bq, bk):
    nq = seq // bq
    r = bq // bk

    def body(q_ref, k_ref, v_ref, o_ref):
        row = lax.broadcasted_iota(jnp.int32, (bk, bk), 0)
        col = lax.broadcasted_iota(jnp.int32, (bk, bk), 1)
        causal = col <= row

        for qi in range(nq):
            m = [jnp.full((bk, 1), -jnp.inf, F32) for _ in range(r)]
            l = [jnp.zeros((bk, 1), F32) for _ in range(r)]
            acc = [jnp.zeros((bk, v_w), F32) for _ in range(r)]

            def sweep(d0, key0, masked):
                q_rows = q_ref[qi * bq + d0 * bk:(qi + 1) * bq, :]
                s = _dot_nt(q_rows, k_ref[key0:key0 + bk, :])
                ps, alphas = [], []
                for d in range(d0, r):
                    sd = s[(d - d0) * bk:(d - d0 + 1) * bk, :]
                    if masked and d == d0:
                        sd = jnp.where(causal, sd, -jnp.inf)
                    m_new = jnp.maximum(m[d], jnp.max(sd, axis=-1, keepdims=True))
                    alpha = jnp.exp(m[d] - m_new)
                    p = jnp.exp(sd - m_new)
                    l[d] = alpha * l[d] + jnp.sum(p, axis=-1, keepdims=True)
                    m[d] = m_new
                    ps.append(p.astype(BF16))
                    alphas.append(alpha)
                pv = _dot(jnp.concatenate(ps, axis=0), v_ref[key0:key0 + bk, :])
                for d in range(d0, r):
                    acc[d] = alphas[d - d0] * acc[d] + pv[(d - d0) * bk:(d - d0 + 1) * bk, :]

            for j in range(qi * r):
                sweep(0, j * bk, False)
            for d0 in range(r):
                sweep(d0, qi * bq + d0 * bk, True)
            for d in range(r):
                o_ref[qi * bq + d * bk:qi * bq + (d + 1) * bk, :] = (acc[d] / l[d]).astype(o_ref.dtype)

    return pl.pallas_call(
        body,
        grid=(n_b, n_heads),
        in_specs=[
            pl.BlockSpec((seq, qk_w), lambda b, h: (b, h)),
            pl.BlockSpec((seq, qk_w), lambda b, h: (b, h)),
            pl.BlockSpec((seq, v_w), lambda b, h: (b, h)),
        ],
        out_specs=pl.BlockSpec((seq, v_w), lambda b, h: (b, h)),
        out_shape=jax.ShapeDtypeStruct((n_b * seq, n_heads * v_w), BF16),
        compiler_params=_params(("arbitrary", "arbitrary")),
        name=name,
    )(q, k, v)


def _decode_attention(name, ql, qp, c_new, k_new, cache_ckv, cache_kpe_t, page_table, dec_seq):
    n_b, n_r, c_w = ql.shape
    p_w = qp.shape[2]
    page = cache_ckv.shape[1]
    n_pages = page_table.shape[1]
    g = math.gcd(n_pages, PAGES_PER_STEP)
    n_ch = math.gcd(g, DECODE_CHAINS)
    gc = g // n_ch
    n_steps = n_pages // g
    n_total = n_b * n_steps
    n_new = c_new.shape[1]
    look = DECODE_SLOTS - 1

    def body(pt_ref, ql_ref, qp_ref, cn_ref, kn_ref, ckv_hbm, kpe_hbm, o_ref,
             cbuf, kbuf, sem, m_scr, l_scr, acc_scr):
        t = pl.program_id(0)
        s_i = lax.rem(t, n_steps)
        slot = lax.rem(t, DECODE_SLOTS)

        def copies(step, slot_):
            cps = []
            for idx in range(g):
                pg = pt_ref[step * g + idx]
                cps.append(pltpu.make_async_copy(ckv_hbm.at[pg], cbuf.at[slot_, idx], sem.at[slot_, 0]))
                cps.append(pltpu.make_async_copy(kpe_hbm.at[pg], kbuf.at[slot_, idx], sem.at[slot_, 1]))
            return cps

        @pl.when(t == 0)
        def _():
            for ahead in range(look):
                for cp in copies(min(ahead, n_total - 1), ahead):
                    cp.start()

        @pl.when(s_i == 0)
        def _():
            m_scr[...] = jnp.full(m_scr.shape, -jnp.inf, F32)
            l_scr[...] = jnp.zeros(l_scr.shape, F32)
            acc_scr[...] = jnp.zeros(acc_scr.shape, F32)

        for cp in copies(t, slot):
            cp.wait()

        q_l = ql_ref[0]
        q_p = qp_ref[0]

        def update(ch, s, c_bf):
            m_prev = m_scr[ch]
            m_new = jnp.maximum(m_prev, jnp.max(s, axis=-1, keepdims=True))
            alpha = jnp.exp(m_prev - m_new)
            p = jnp.exp(s - m_new)
            l_scr[ch] = alpha * l_scr[ch] + jnp.sum(p, axis=-1, keepdims=True)
            acc_scr[ch] = alpha * acc_scr[ch] + _dot(p.astype(BF16), c_bf)
            m_scr[ch] = m_new

        for ch in range(n_ch):
            c_bf = jnp.concatenate([cbuf[slot, idx].astype(BF16) for idx in range(ch * gc, (ch + 1) * gc)], axis=0)
            kt_bf = jnp.concatenate([kbuf[slot, idx].astype(BF16) for idx in range(ch * gc, (ch + 1) * gc)], axis=1)
            update(ch, _dot_nt(q_l, c_bf) + _dot(q_p, kt_bf), c_bf)

        for cp in copies(jnp.minimum(t + look, n_total - 1), lax.rem(t + look, DECODE_SLOTS)):
            cp.start()

        @pl.when(s_i == n_steps - 1)
        def _():
            cn = cn_ref[0].astype(BF16)
            kn = kn_ref[0].astype(BF16)
            s = _dot_nt(q_l, cn) + _dot_nt(q_p, kn)
            tok = lax.broadcasted_iota(jnp.int32, (n_r, n_new), 0) % dec_seq
            key = lax.broadcasted_iota(jnp.int32, (n_r, n_new), 1)
            update(0, jnp.where(key <= tok, s, -jnp.inf), cn)
            m_all = m_scr[0]
            for ch in range(1, n_ch):
                m_all = jnp.maximum(m_all, m_scr[ch])
            l_all = jnp.zeros_like(m_all)
            acc_all = jnp.zeros((n_r, c_w), F32)
            for ch in range(n_ch):
                w = jnp.exp(m_scr[ch] - m_all)
                l_all = l_all + w * l_scr[ch]
                acc_all = acc_all + w * acc_scr[ch]
            o_ref[0] = acc_all / l_all

        @pl.when(t == n_total - 1)
        def _():
            for ahead in range(1, DECODE_SLOTS):
                for cp in copies(n_total - 1, lax.rem(t + ahead, DECODE_SLOTS)):
                    cp.wait()

    grid_spec = pltpu.PrefetchScalarGridSpec(
        num_scalar_prefetch=1,
        grid=(n_total,),
        in_specs=[
            pl.BlockSpec((1, n_r, c_w), lambda t, pt: (t // n_steps, 0, 0)),
            pl.BlockSpec((1, n_r, p_w), lambda t, pt: (t // n_steps, 0, 0)),
            pl.BlockSpec((1, n_new, c_w), lambda t, pt: (t // n_steps, 0, 0)),
            pl.BlockSpec((1, n_new, p_w), lambda t, pt: (t // n_steps, 0, 0)),
            pl.BlockSpec(memory_space=pl.ANY),
            pl.BlockSpec(memory_space=pl.ANY),
        ],
        out_specs=pl.BlockSpec((1, n_r, c_w), lambda t, pt: (t // n_steps, 0, 0)),
        scratch_shapes=[pltpu.VMEM((DECODE_SLOTS, g, page, c_w), F32),
                        pltpu.VMEM((DECODE_SLOTS, g, p_w, page), F32),
                        pltpu.SemaphoreType.DMA((DECODE_SLOTS, 2)),
                        pltpu.VMEM((n_ch, n_r, 1), F32), pltpu.VMEM((n_ch, n_r, 1), F32),
                        pltpu.VMEM((n_ch, n_r, c_w), F32)],
    )
    return pl.pallas_call(
        body,
        grid_spec=grid_spec,
        out_shape=jax.ShapeDtypeStruct((n_b, n_r, c_w), F32),
        compiler_params=_params(("arbitrary",)),
        name=name,
    )(page_table.reshape(-1), ql, qp, c_new, k_new, cache_ckv, cache_kpe_t)


def _head_matmul(name, x, w, out_dtype):
    n_h, m, k = x.shape
    n = w.shape[2]

    def body(x_ref, w_ref, o_ref):
        o_ref[0] = _dot(x_ref[0].astype(BF16), w_ref[0].astype(BF16)).astype(o_ref.dtype)

    return pl.pallas_call(
        body,
        grid=(n_h,),
        in_specs=[pl.BlockSpec((1, m, k), lambda h: (h, 0, 0)), pl.BlockSpec((1, k, n), lambda h: (h, 0, 0))],
        out_specs=pl.BlockSpec((1, m, n), lambda h: (h, 0, 0)),
        out_shape=jax.ShapeDtypeStruct((n_h, m, n), out_dtype),
        compiler_params=_params(("arbitrary",)),
        name=name,
    )(x, w)


def kernel(x_prompt, x_sample, state_hgrn, state_conv, cache_ckv, cache_kpe, page_table, p_prompt, p_sample, attn_norm, ffn_norm, ple_norm, final_norm, hg_wq, hg_wf, hg_wi, hg_wg, hg_onorm, hg_wo, hg_lb_raw, kv_norm, w_dkv, ckv_norm, w_uk, w_uv, q_wd, q_norm, q_wu, mla_wo, ffn_wgate, ffn_wup, ffn_conv_w, ffn_conv_b, ffn_wdown, ple_wg, ple_we):
    n_b, seq, d = x_prompt.shape
    n_bd, dec_seq, _ = x_sample.shape
    depth = attn_norm.shape[0]
    n_a = hg_wq.shape[0]
    dff = ffn_wgate.shape[2]
    ple_dim = p_prompt.shape[3]
    kv_lora, mla_heads, nope = w_uk.shape
    v_dim = w_uv.shape[2]
    rope_dim = w_dkv.shape[1] - kv_lora
    half = rope_dim // 2
    q_lora = q_wd.shape[2]
    page = cache_ckv.shape[1]
    past_len = page_table.shape[1] * page
    hg_heads = hg_wq.shape[2] // HG_DK
    scale = float(nope + rope_dim) ** -0.5
    qk_w = 2 * LANES
    assert nope == LANES and v_dim == LANES and rope_dim <= LANES and rope_dim % 2 == 0

    mp, ms = n_b * seq, n_bd * dec_seq
    m = mp + ms
    tm = ms
    assert tm % BF16_ROWS == 0 and seq % tm == 0 and 2 <= dec_seq <= SUBLANES
    n_prompt_tiles = mp // tm
    tiles_per_seq = seq // tm
    tmt = _pick_row_tile(m)

    h = jnp.concatenate([x_prompt.reshape(mp, d), x_sample.reshape(ms, d)], axis=0)
    p_all = jnp.concatenate([p_prompt.reshape(depth, mp, ple_dim), p_sample.reshape(depth, ms, ple_dim)], axis=1)

    pos = jnp.concatenate([jnp.tile(jnp.arange(seq, dtype=jnp.int32), n_b),
                           jnp.tile(past_len + jnp.arange(dec_seq, dtype=jnp.int32), n_bd)])
    inv = ROPE_THETA ** (-jnp.arange(half, dtype=F32) / half)
    ang = pos.astype(F32)[:, None] * inv[None, :]
    cos2 = jnp.concatenate([jnp.cos(ang), jnp.cos(ang)], axis=1)
    sin2 = jnp.concatenate([-jnp.sin(ang), jnp.sin(ang)], axis=1)
    cos_pad = jnp.pad(cos2, ((0, 0), (0, LANES - rope_dim)))
    sin_pad = jnp.pad(sin2, ((0, 0), (0, LANES - rope_dim)))

    wdown_bf = ffn_wdown.astype(BF16)

    states_p, states_s, conv_p, conv_s = [], [], [], []
    c_all = kpe_all = None

    for layer in range(depth):
        if layer < n_a:
            xn = _rmsnorm(h, attn_norm[layer], BF16, f"attn_norm_{layer}")
            hw = hg_heads * HG_DK

            def qf_epilogue(accs, rows, cols, layer=layer):
                raw = cols[0]
                e = jnp.exp(raw - jnp.max(raw, axis=0, keepdims=True))
                lb = jnp.sum(e[:layer + 1], axis=0, keepdims=True) / jnp.sum(e, axis=0, keepdims=True)
                return jax.nn.silu(accs[0]), lb + (1.0 - lb) * jax.nn.sigmoid(accs[1])

            n_j = _pick_n_tiles([hw, hw], [d, d], d * 2, tmt, 2 * 4)
            q_act, f_act = _fused_matmul(
                f"hgrn_qf_{layer}", [xn], [(hg_wq, layer), (hg_wf, layer)], [0, 0], [], [hg_lb_raw.astype(F32)],
                qf_epilogue, [(F32, hw), (F32, hw)], tmt, n_j)
            v_act, g_act = _fused_matmul(
                f"hgrn_ig_{layer}", [xn], [(hg_wi, layer), (hg_wg, layer)], [0, 0], [], [],
                lambda accs, rows, cols: (accs[0], jax.nn.silu(accs[1])), [(F32, hw), (F32, hw)], tmt, n_j)

            hb = math.gcd(hg_heads, 16)
            n_ser = math.gcd(seq // HG_CHUNK, 4)
            o_p, s_p = _gla(f"gla_prompt_{layer}", q_act, f_act, v_act, g_act, hg_onorm[layer],
                            jnp.zeros((n_b, hg_heads, HG_DK, HG_DK), F32), n_b, seq, HG_CHUNK, 1, n_ser,
                            math.gcd(hg_heads, 32))

            def pad_seq(a, fill):
                a = a[mp:].reshape(n_bd, dec_seq, hw)
                a = jnp.pad(a, ((0, 0), (0, SUBLANES - dec_seq), (0, 0)), constant_values=fill)
                return a.reshape(n_bd * SUBLANES, hw)

            o_s, s_s = _gla(f"gla_sample_{layer}", pad_seq(q_act, 0.0), pad_seq(f_act, 1.0), pad_seq(v_act, 0.0),
                            pad_seq(g_act, 0.0), hg_onorm[layer], state_hgrn[layer].astype(F32),
                            n_bd, SUBLANES, SUBLANES, math.gcd(n_bd, 4), 1, hb)
            o_s = o_s.reshape(n_bd, SUBLANES, hw)[:, :dec_seq].reshape(ms, hw)
            o_all = jnp.concatenate([o_p, o_s], axis=0)
            states_p.append(s_p)
            states_s.append(s_s.astype(state_hgrn.dtype))
            w_out = (hg_wo, layer)
        else:
            j = layer - n_a
            tmq = math.gcd(tm, 256)
            n_jq = _pick_n_tiles([q_lora], [d], d * 4, tmq, 2, align=q_lora)

            def cq_epilogue(accs, rows, cols):
                x = accs[0]
                return (x * lax.rsqrt(jnp.mean(x * x, axis=-1, keepdims=True) + EPS) * cols[0],)

            assert n_jq == 1
            cq = _fused_matmul(f"mla_qdown_{j}", [h], [(q_wd, j)], [0], [], [q_norm[j].reshape(1, q_lora).astype(F32)],
                               cq_epilogue, [(BF16, q_lora)], tmq, n_jq, norm_gain=attn_norm[layer])[0]

            wq3 = q_wu[j].reshape(q_lora, mla_heads, nope + rope_dim)
            wq_pad = jnp.pad(wq3, ((0, 0), (0, 0), (0, qk_w - nope - rope_dim))).reshape(q_lora, mla_heads * qk_w)
            n_jh = _pick_n_tiles([mla_heads * qk_w], [q_lora], q_lora * 2, tmt, 2, align=qk_w)
            heads_per_tile = mla_heads // n_jh

            def q_epilogue(accs, rows, cols):
                cos_t, sin_t = rows
                lane = lax.broadcasted_iota(jnp.int32, cos_t.shape, 1)
                outs = []
                for hh in range(heads_per_tile):
                    lo = accs[0][:, hh * qk_w:hh * qk_w + LANES]
                    hi = accs[0][:, hh * qk_w + LANES:(hh + 1) * qk_w]
                    swapped = jnp.where(lane < half, pltpu.roll(hi, LANES - half, 1), pltpu.roll(hi, half, 1))
                    outs += [lo * scale, (hi * cos_t + swapped * sin_t) * scale]
                return (jnp.concatenate(outs, axis=1),)

            qfull = _fused_matmul(f"mla_qup_{j}", [cq], [(wq_pad, None)], [0], [(cos_pad, False), (sin_pad, False)], [],
                                  q_epilogue, [(BF16, mla_heads * qk_w)], tmt, n_jh)[0]

            kpe_pad = jnp.pad(kpe_all[:mp], ((0, 0), (0, LANES - rope_dim)))
            heads_k = math.gcd(mla_heads, 8)
            n_jk = mla_heads // heads_k

            def kv_epilogue(accs, rows, cols):
                ks = []
                for hh in range(heads_k):
                    ks += [accs[0][:, hh * nope:(hh + 1) * nope], rows[0]]
                return jnp.concatenate(ks, axis=1), accs[1]

            tmk = math.gcd(mp, 1024)
            kfull, vfull = _fused_matmul(
                f"mla_kv_up_{j}", [c_all],
                [(w_uk.reshape(kv_lora, mla_heads * nope), None), (w_uv.reshape(kv_lora, mla_heads * v_dim), None)],
                [0, 0], [(kpe_pad, False)], [], kv_epilogue,
                [(BF16, mla_heads * qk_w), (BF16, mla_heads * v_dim)], tmk, n_jk, m_rows=mp)
            bq = math.gcd(seq, 512)
            o_p = _flash_prompt(f"mla_prompt_attn_{j}", qfull, kfull, vfull, n_b, seq, mla_heads, qk_w, v_dim,
                                bq, math.gcd(bq, 256))

            q_s = qfull[mp:].reshape(n_bd, dec_seq, mla_heads, qk_w)
            q_nope_h = q_s[..., :nope].transpose(2, 0, 1, 3).reshape(mla_heads, ms, nope)
            q_lat = _head_matmul(f"mla_q_absorb_{j}", q_nope_h, w_uk.transpose(1, 2, 0), BF16)
            ql = q_lat.reshape(mla_heads, n_bd, dec_seq, kv_lora).transpose(1, 0, 2, 3).reshape(n_bd, mla_heads * dec_seq, kv_lora)
            qp = q_s[..., nope:nope + rope_dim].transpose(0, 2, 1, 3).reshape(n_bd, mla_heads * dec_seq, rope_dim)
            pad_new = ((0, 0), (0, BF16_ROWS - dec_seq), (0, 0))
            c_new = jnp.pad(c_all[mp:].reshape(n_bd, dec_seq, kv_lora), pad_new)
            k_new = jnp.pad(kpe_all[mp:].reshape(n_bd, dec_seq, rope_dim), pad_new)
            lat = _decode_attention(f"mla_sample_attn_{j}", ql, qp, c_new, k_new, cache_ckv,
                                    jnp.transpose(cache_kpe, (0, 2, 1)), page_table.astype(jnp.int32), dec_seq)
            lat_h = lat.reshape(n_bd, mla_heads, dec_seq, kv_lora).transpose(1, 0, 2, 3).reshape(mla_heads, ms, kv_lora)
            o_sh = _head_matmul(f"mla_v_up_{j}", lat_h, w_uv.transpose(1, 0, 2), BF16)
            o_s = o_sh.transpose(1, 0, 2).reshape(ms, mla_heads * v_dim)
            o_all = jnp.concatenate([o_p, o_s], axis=0)
            w_out = (mla_wo, j)

        k_out = w_out[0].shape[-2]
        n_jo = _pick_n_tiles([d], [k_out], k_out * 2, tmt, 2 * 4)
        h = _fused_matmul(f"mixer_out_{layer}", [o_all], [w_out], [0], [(h, True)], [],
                          lambda accs, rows, cols: (rows[0] + accs[0],), [(F32, d)], tmt, n_jo)[0]

        xn = _rmsnorm(h, ffn_norm[layer], BF16, f"ffn_norm_{layer}")
        buf = state_conv[layer].astype(F32)
        e1 = jnp.pad(buf[:, 1:2], ((0, 0), (0, dec_seq - 1), (0, 0))).reshape(ms, dff)
        e2 = jnp.pad(buf, ((0, 0), (0, dec_seq - 2), (0, 0))).reshape(ms, dff)
        n_jf = _pick_n_tiles([dff, dff], [d, d], d * 2, tm, 4 * 4)
        hid, tails, a_samp = _ffn_up(f"ffn_up_{layer}", xn, (ffn_wgate, layer), (ffn_wup, layer), ffn_conv_w[layer],
                                     ffn_conv_b[layer], e1, e2, n_prompt_tiles, tiles_per_seq, dec_seq, tm, n_jf)
        conv_p.append(tails[tiles_per_seq - 1:n_prompt_tiles:tiles_per_seq, SUBLANES - 2:, :].astype(state_conv.dtype))
        conv_s.append(a_samp.reshape(n_bd, dec_seq, dff)[:, dec_seq - 2:, :].astype(state_conv.dtype))
        tmd = math.gcd(tm, 256)
        n_jd = _pick_n_tiles([d], [dff], dff * 2, tmd, 2 * 4, bf16_weights=True)
        h = _fused_matmul(f"ffn_down_{layer}", [hid], [(wdown_bf, layer)], [0], [(h, True)], [],
                          lambda accs, rows, cols: (rows[0] + accs[0],), [(F32, d)], tmd, n_jd)[0]

        xn = _rmsnorm(h, ple_norm[layer], BF16, f"ple_norm_{layer}")
        n_jp = _pick_n_tiles([d, d], [d, ple_dim], (d + 2 * ple_dim) * 2, tm, 2 * 4)
        h = _fused_matmul(f"ple_{layer}", [xn, p_all[layer]], [(ple_wg, layer), (ple_we, layer)], [0, 1], [(h, True)], [],
                          lambda accs, rows, cols: (rows[0] + jax.nn.sigmoid(accs[0]) * accs[1],),
                          [(F32, d)], tm, n_jp)[0]

        if layer == n_a - 1:
            swap = jnp.concatenate([jnp.arange(half, rope_dim), jnp.arange(half)])
            w_pe = w_dkv[:, kv_lora:]
            tmq_kv = math.gcd(tm, 256)

            def kvs_epilogue(accs, rows, cols):
                x = accs[0]
                c = x * lax.rsqrt(jnp.mean(x * x, axis=-1, keepdims=True) + EPS) * cols[0]
                return c, accs[1] * rows[0] + accs[2] * rows[1]

            c_all, kpe_all = _fused_matmul(
                "kv_stream", [h], [(w_dkv[:, :kv_lora], None), (w_pe, None), (w_pe[:, swap], None)], [0, 0, 0],
                [(cos2, False), (sin2, False)], [ckv_norm.reshape(1, kv_lora).astype(F32)],
                kvs_epilogue, [(F32, kv_lora), (F32, rope_dim)], tmq_kv, 1, norm_gain=kv_norm)

    y_p = _rmsnorm(h, final_norm, x_prompt.dtype, "final_norm_prompt", 0, mp)
    y_s = _rmsnorm(h, final_norm, x_sample.dtype, "final_norm_sample", mp, ms)
    return (y_p.reshape(n_b, seq, d), y_s.reshape(n_bd, dec_seq, d),
            jnp.stack(states_p).astype(state_hgrn.dtype), jnp.stack(states_s),
            jnp.stack(conv_p), jnp.stack(conv_s),
            c_all[:mp].reshape(n_b, seq, kv_lora), c_all[mp:].reshape(n_bd, dec_seq, kv_lora),
            kpe_all[:mp].reshape(n_b, seq, rope_dim), kpe_all[mp:].reshape(n_bd, dec_seq, rope_dim))
```
